```python
import math
import jax, jax.numpy as jnp
from jax import lax
import numpy as np

D_MODEL = 2048
BATCH = 4
SEQ = 4096
DEPTH = 4

GRID_W = 64
CTX_LEN = 256
F32 = jnp.float32

DA_HEADS = 8
DA_QK_DIM = 64
DA_V_DIM = 2 * DA_QK_DIM
DA_QK_WIDTH = DA_HEADS * 2 * DA_QK_DIM
DA_WIDTH = DA_HEADS * DA_V_DIM
AX_DIM = DA_QK_DIM // 2
ROPE_BASE = 10000.0
Q_BLOCK = 128

S5_WIDTH = 1024
S5_GROUP = 16
S5_GROUPS = S5_WIDTH // S5_GROUP
S5_STATE = 64

GLA_HEADS = 4
GLA_DK = 128
GLA_DV = 256
GLA_WIDTH = GLA_HEADS * GLA_DV
GLA_RANK = 16
GLA_CHUNK = 64
GLA_GATE_NORM = 16.0

N_BRANCH = 3
BRANCH_W = 1024

PEER_HEADS = 8
PEER_NKEYS = 128
PEER_EXPERTS = PEER_NKEYS * PEER_NKEYS
PEER_QDIM = 256
PEER_TOPK = 16
PEER_TOK_BLOCK = 64

IN_SIZES = (DA_QK_WIDTH, DA_QK_WIDTH, DA_WIDTH,
            S5_WIDTH,
            GLA_HEADS * GLA_DK, GLA_HEADS * GLA_DK, GLA_WIDTH,
            GLA_RANK, GLA_RANK,
            GLA_WIDTH,
            N_BRANCH * D_MODEL)
N_IN = sum(IN_SIZES)

DEEPNORM_ALPHA = (2 * DEPTH) ** 0.25
DEEPNORM_BETA = (8 * DEPTH) ** -0.25
LN_EPS = 1e-5

kernel_name = 'hybrid_diffusion_trunk_diffattn_s5_gla_peer'


def layer_norm(x, g, b):
    xf = x.astype(F32)
    mu = jnp.mean(xf, -1, keepdims=True)
    var = jnp.mean(jnp.square(xf - mu), -1, keepdims=True)
    y = (xf - mu) * lax.rsqrt(var + LN_EPS) * g.astype(F32) + b.astype(F32)
    return y.astype(x.dtype)


def rms_norm(x, g):
    xf = x.astype(F32)
    y = xf * lax.rsqrt(jnp.mean(jnp.square(xf), -1, keepdims=True) + LN_EPS) * g.astype(F32)
    return y.astype(x.dtype)


def split_projection(p):
    return jnp.split(p, [int(s) for s in np.cumsum(IN_SIZES)[:-1]], axis=-1)


def axial_rope_tables(n_tokens):
    rows = n_tokens // GRID_W
    r, c = jnp.meshgrid(jnp.arange(rows), jnp.arange(GRID_W), indexing='ij')
    pos = jnp.stack([r.reshape(-1), c.reshape(-1)], axis=-1).astype(F32)
    inv = ROPE_BASE ** (-jnp.arange(AX_DIM // 2, dtype=F32) * 2.0 / AX_DIM)
    ang = pos[:, :, None] * inv
    return jnp.cos(ang), jnp.sin(ang)


def rope_2d(x, cos, sin):
    shp = x.shape
    xr = x.reshape(shp[:-1] + (2, 2, AX_DIM // 2))
    x1, x2 = xr[..., 0, :], xr[..., 1, :]
    cs, sn = cos[None, :, None, None], sin[None, :, None, None]
    return jnp.stack([x1 * cs - x2 * sn, x2 * cs + x1 * sn], axis=-2).reshape(shp)


def diff_attention(q_c, k_c, v_c, q_l, k_l, v_l, lam_q1, lam_k1, lam_q2, lam_k2, norm_g, lam_init, need_ctx):
    B, S = q_l.shape[:2]
    def heads_qk(t):
        return t.astype(F32).reshape(t.shape[:2] + (DA_HEADS, 2, DA_QK_DIM))
    def heads_v(t):
        return t.astype(F32).reshape(t.shape[:2] + (DA_HEADS, DA_V_DIM))
    cos, sin = axial_rope_tables(S)
    qc, kc, vc = heads_qk(q_c), heads_qk(k_c), heads_v(v_c)
    ql = rope_2d(heads_qk(q_l), cos, sin)
    kl = rope_2d(heads_qk(k_l), cos, sin)
    vl = heads_v(v_l)
    lam = (jnp.exp(jnp.sum(lam_q1.astype(F32) * lam_k1.astype(F32)))
           - jnp.exp(jnp.sum(lam_q2.astype(F32) * lam_k2.astype(F32))) + lam_init)
    scale = DA_QK_DIM ** -0.5

    def attend(q, k, v):
        p = jax.nn.softmax(jnp.einsum('bqhcd,bkhcd->bhcqk', q, k) * scale, axis=-1)
        return jnp.einsum('bhqk,bkhd->bqhd', p[:, :, 0] - lam * p[:, :, 1], v)

    k_all = jnp.concatenate([kc, kl], axis=1)
    v_all = jnp.concatenate([vc, vl], axis=1)
    qb = jnp.moveaxis(ql.reshape((B, S // Q_BLOCK, Q_BLOCK) + ql.shape[2:]), 1, 0)
    ol = lax.map(lambda blk: attend(blk, k_all, v_all), qb)
    ol = jnp.moveaxis(ol, 0, 1).reshape(B, S, DA_HEADS, DA_V_DIM)

    def post(o):
        return (rms_norm(o, norm_g) * (1.0 - lam_init)).reshape(o.shape[:2] + (DA_WIDTH,))
    o_c = post(attend(qc, kc, vc)) if need_ctx else None
    return o_c, post(ol)


def s5_discretize(a_re, a_im, log_step, b_re, b_im):
    dt = jnp.exp(log_step.astype(F32))[:, None]
    ar, ai = a_re.astype(F32), a_im.astype(F32)
    mag = jnp.exp(ar * dt)
    lb_re, lb_im = mag * jnp.cos(ai * dt), mag * jnp.sin(ai * dt)
    nr, ni = lb_re - 1.0, lb_im
    den = ar * ar + ai * ai
    f_re = (nr * ar + ni * ai) / den
    f_im = (ni * ar - nr * ai) / den
    br, bi = b_re.astype(F32), b_im.astype(F32)
    bb_re = f_re[..., None] * br - f_im[..., None] * bi
    bb_im = f_re[..., None] * bi + f_im[..., None] * br
    return lb_re, lb_im, bb_re, bb_im


def s5_scan(lb_re, lb_im, bu_re, bu_im, x0_re, x0_im, reverse):
    first = -1 if reverse else 0
    bu_re = bu_re.at[:, first].add(lb_re * x0_re - lb_im * x0_im)
    bu_im = bu_im.at[:, first].add(lb_re * x0_im + lb_im * x0_re)
    n = bu_re.shape[1]
    a_re = jnp.broadcast_to(lb_re, (1, n) + lb_re.shape)
    a_im = jnp.broadcast_to(lb_im, (1, n) + lb_im.shape)

    def combine(e1, e2):
        a1r, a1i, b1r, b1i = e1
        a2r, a2i, b2r, b2i = e2
        return (a2r * a1r - a2i * a1i, a2r * a1i + a2i * a1r,
                a2r * b1r - a2i * b1i + b2r, a2r * b1i + a2i * b1r + b2i)

    _, _, xr, xi = lax.associative_scan(combine, (a_re, a_im, bu_re, bu_im), reverse=reverse, axis=1)
    return xr, xi


def s5_mixer(u_c, u_l, a_re, a_im, log_step, b_re, b_im, c_re, c_im, d, glu_w, glu_b, need_ctx):
    def groups(u):
        return u.astype(F32).reshape(u.shape[:2] + (S5_GROUPS, S5_GROUP))
    uc, ul = groups(u_c), groups(u_l)
    dg = d.astype(F32).reshape(S5_GROUPS, S5_GROUP)
    y_c, y_l = uc * dg, ul * dg
    zeros = jnp.zeros((u_l.shape[0], S5_GROUPS, S5_STATE), F32)
    for di, reverse in enumerate((False, True)):
        lb_re, lb_im, bb_re, bb_im = s5_discretize(a_re[di], a_im[di], log_step[di], b_re[di], b_im[di])
        cr, ci = c_re[di].astype(F32), c_im[di].astype(F32)
        def drive(u):
            return (jnp.einsum('gpi,bngi->bngp', bb_re, u), jnp.einsum('gpi,bngi->bngp', bb_im, u))
        def readout(xr, xi):
            return jnp.einsum('gip,bngp->bngi', cr, xr) - jnp.einsum('gip,bngp->bngi', ci, xi)
        xc_re, xc_im = s5_scan(lb_re, lb_im, *drive(uc), zeros, zeros, reverse)
        end = 0 if reverse else -1
        xl_re, xl_im = s5_scan(lb_re, lb_im, *drive(ul), xc_re[:, end], xc_im[:, end], reverse)
        y_l = y_l + readout(xl_re, xl_im)
        if need_ctx:
            y_c = y_c + readout(xc_re, xc_im)

    def glu(y):
        z = jax.nn.gelu(y.reshape(y.shape[:2] + (S5_WIDTH,)))
        return z * jax.nn.sigmoid(z @ glu_w.astype(F32) + glu_b.astype(F32))
    return (glu(y_c) if need_ctx else None), glu(y_l)


def gla_chunked(q, k, v, log_a, s0):
    B, n, H, dk = q.shape
    nc = n // GLA_CHUNK
    def chunks(t):
        return t.reshape(B, nc, GLA_CHUNK, H, t.shape[-1]).transpose(1, 0, 3, 2, 4)
    qc, kc, vc, gc = chunks(q), chunks(k), chunks(v), chunks(log_a)
    bcum = jnp.cumsum(gc, axis=3)
    blast = bcum[:, :, :, -1:]
    q_t = qc * jnp.exp(bcum)
    k_t = kc * jnp.exp(-bcum)
    k_s = kc * jnp.exp(blast - bcum)
    mask = jnp.tril(jnp.ones((GLA_CHUNK, GLA_CHUNK), dtype=bool))
    attn = jnp.where(mask, jnp.einsum('nbhtd,nbhsd->nbhts', q_t, k_t), 0.0)
    o_intra = jnp.einsum('nbhts,nbhsv->nbhtv', attn, vc)

    def step(state, xs):
        qt, ks, vch, bl = xs
        o_inter = jnp.einsum('bhtd,bhdv->bhtv', qt, state)
        state = state * jnp.exp(bl[:, :, 0, :, None]) + jnp.einsum('bhsd,bhsv->bhdv', ks, vch)
        return state, o_inter

    s_final, o_inter = lax.scan(step, s0, (q_t, k_s, vc, blast))
    o = (o_intra + o_inter).transpose(1, 0, 3, 2, 4).reshape(B, n, H, v.shape[-1])
    return o, s_final


def gla_mixer(parts_c, parts_l, gate_up, gate_b, norm_g, need_ctx):
    def prep(q, k, v, g_f, g_b):
        B, n = q.shape[:2]
        def heads(t, dd):
            return t.astype(F32).reshape(B, n, GLA_HEADS, dd)
        def log_gate(g, di):
            logits = g.astype(F32) @ gate_up[di].astype(F32) + gate_b[di].astype(F32)
            return heads(jax.nn.log_sigmoid(logits), GLA_DK) / GLA_GATE_NORM
        return (heads(q, GLA_DK) * GLA_DK ** -0.5, heads(k, GLA_DK), heads(v, GLA_DV),
                log_gate(g_f, 0), log_gate(g_b, 1))
    qc, kc, vc, afc, abc = prep(*parts_c[:5])
    ql, kl, vl, afl, abl = prep(*parts_l[:5])
    s0 = jnp.zeros((ql.shape[0], GLA_HEADS, GLA_DK, GLA_DV), F32)
    def flip(t):
        return t[:, ::-1]
    oc_f, sc_f = gla_chunked(qc, kc, vc, afc, s0)
    oc_b, sc_b = gla_chunked(flip(qc), flip(kc), flip(vc), flip(abc), s0)
    ol_f, _ = gla_chunked(ql, kl, vl, afl, sc_f)
    ol_b, _ = gla_chunked(flip(ql), flip(kl), flip(vl), flip(abl), sc_b)

    def post(o, r):
        o = rms_norm(o, norm_g)
        return o.reshape(o.shape[:2] + (GLA_WIDTH,)) * jax.nn.silu(r.astype(F32))
    o_c = post(oc_f + flip(oc_b), parts_c[5]) if need_ctx else None
    return o_c, post(ol_f + flip(ol_b), parts_l[5])


def token_mixer(hc, hl, w_in, lam_q1, lam_k1, lam_q2, lam_k2, da_norm_g,
                s5_a_re, s5_a_im, s5_log_step, s5_b_re, s5_b_im, s5_c_re, s5_c_im,
                s5_d, s5_glu_w, s5_glu_b, gla_gate_up, gla_gate_b, gla_norm_g,
                w_branch, w_out, lam_init, need_ctx):
    (qa_c, ka_c, va_c, u_c, gq_c, gk_c, gv_c, gf_c, gb_c, r_c, mg_c) = split_projection(hc @ w_in)
    (qa_l, ka_l, va_l, u_l, gq_l, gk_l, gv_l, gf_l, gb_l, r_l, mg_l) = split_projection(hl @ w_in)
    oa_c, oa_l = diff_attention(qa_c, ka_c, va_c, qa_l, ka_l, va_l,
                                lam_q1, lam_k1, lam_q2, lam_k2, da_norm_g, lam_init, need_ctx)
    ob_c, ob_l = s5_mixer(u_c, u_l, s5_a_re, s5_a_im, s5_log_step, s5_b_re, s5_b_im,
                          s5_c_re, s5_c_im, s5_d, s5_glu_w, s5_glu_b, need_ctx)
    oc_c, oc_l = gla_mixer((gq_c, gk_c, gv_c, gf_c, gb_c, r_c), (gq_l, gk_l, gv_l, gf_l, gb_l, r_l),
                           gla_gate_up, gla_gate_b, gla_norm_g, need_ctx)

    def merge(oa, ob, oc, gates):
        g = gates.reshape(gates.shape[:2] + (N_BRANCH, D_MODEL))
        dt = gates.dtype
        y = (jax.nn.sigmoid(g[:, :, 0]) * (oa.astype(dt) @ w_branch[0])
             + jax.nn.sigmoid(g[:, :, 1]) * (ob.astype(dt) @ w_branch[1])
             + jax.nn.sigmoid(g[:, :, 2]) * (oc.astype(dt) @ w_branch[2]))
        return y @ w_out
    mix_l = merge(oa_l, ob_l, oc_l, mg_l)
    mix_c = merge(oa_c, ob_c, oc_c, mg_c) if need_ctx else None
    return mix_c, mix_l


def peer(h, wq, k1, k2, u_tab, v_tab):
    B, n, D = h.shape
    t = h.reshape(B * n, D)
    T = t.shape[0]
    q = (t @ wq).astype(F32).reshape(T, PEER_HEADS, 2, PEER_QDIM // 2)
    s1 = jnp.einsum('thd,hkd->thk', q[:, :, 0], k1.astype(F32))
    s2 = jnp.einsum('thd,hkd->thk', q[:, :, 1], k2.astype(F32))
    v1, i1 = lax.top_k(s1, PEER_TOPK)
    v2, i2 = lax.top_k(s2, PEER_TOPK)
    n_cand = PEER_TOPK * PEER_TOPK
    cand_s = (v1[..., :, None] + v2[..., None, :]).reshape(T, PEER_HEADS, n_cand)
    cand_i = (i1[..., :, None] * PEER_NKEYS + i2[..., None, :]).reshape(T, PEER_HEADS, n_cand)
    top_s, top_pos = lax.top_k(cand_s, PEER_TOPK)
    idx = jnp.take_along_axis(cand_i, top_pos, axis=-1)
    gate = jax.nn.softmax(top_s, axis=-1).astype(h.dtype)
    nb = T // PEER_TOK_BLOCK

    def block(args):
        tb, ib, gb = args
        act = jax.nn.gelu(jnp.einsum('td,thkd->thk', tb, u_tab[ib]))
        return jnp.einsum('thk,thkd->td', act * gb, v_tab[ib])

    out = lax.map(block, (t.reshape(nb, PEER_TOK_BLOCK, D),
                          idx.reshape(nb, PEER_TOK_BLOCK, PEER_HEADS, PEER_TOPK),
                          gate.reshape(nb, PEER_TOK_BLOCK, PEER_HEADS, PEER_TOPK)))
    return out.reshape(B, n, D)


def setup_inputs(seed: int = 0) -> dict:
    key = jax.random.key(seed)
    ks = iter(jax.random.split(key, 48))
    def nrm(shape, s):
        return jax.random.normal(next(ks), shape, F32) * s
    L, D = DEPTH, D_MODEL
    G, P = S5_GROUPS, S5_STATE
    inp = {}
    inp['x'] = nrm((BATCH, SEQ, D), 1.0)
    inp['c'] = nrm((BATCH, D), 1.0)
    inp['ctx'] = nrm((BATCH, CTX_LEN, D), 1.0)
    inp['c_ctx'] = nrm((D,), 1.0)
    inp['w_ada'] = nrm((L, D, 6 * D), D ** -0.5)
    inp['b_ada'] = nrm((L, 6 * D), 0.01)
    inp['w_in'] = nrm((L, D, N_IN), D ** -0.5)
    inp['lam_q1'] = nrm((L, DA_QK_DIM), 0.1)
    inp['lam_k1'] = nrm((L, DA_QK_DIM), 0.1)
    inp['lam_q2'] = nrm((L, DA_QK_DIM), 0.1)
    inp['lam_k2'] = nrm((L, DA_QK_DIM), 0.1)
    inp['da_norm_g'] = 1.0 + nrm((L, DA_V_DIM), 0.01)
    inp['s5_a_re'] = -0.5 + nrm((L, 2, G, P), 0.01)
    inp['s5_a_im'] = math.pi * jnp.arange(P, dtype=F32) + nrm((L, 2, G, P), 0.01)
    inp['s5_log_step'] = jax.random.uniform(next(ks), (L, 2, G), F32, math.log(1e-3), math.log(1e-1))
    inp['s5_b_re'] = nrm((L, 2, G, P, S5_GROUP), (2 * S5_GROUP) ** -0.5)
    inp['s5_b_im'] = nrm((L, 2, G, P, S5_GROUP), (2 * S5_GROUP) ** -0.5)
    inp['s5_c_re'] = nrm((L, 2, G, S5_GROUP, P), P ** -0.5)
    inp['s5_c_im'] = nrm((L, 2, G, S5_GROUP, P), P ** -0.5)
    inp['s5_d'] = nrm((L, S5_WIDTH), 1.0)
    inp['s5_glu_w'] = nrm((L, S5_WIDTH, S5_WIDTH), S5_WIDTH ** -0.5)
    inp['s5_glu_b'] = nrm((L, S5_WIDTH), 0.01)
    inp['gla_gate_up'] = nrm((L, 2, GLA_RANK, GLA_HEADS * GLA_DK), GLA_RANK ** -0.5)
    inp['gla_gate_b'] = nrm((L, 2, GLA_HEADS * GLA_DK), 0.01)
    inp['gla_norm_g'] = 1.0 + nrm((L, GLA_DV), 0.01)
    inp['w_branch'] = nrm((L, N_BRANCH, BRANCH_W, D), BRANCH_W ** -0.5)
    inp['w_out'] = nrm((L, D, D), DEEPNORM_BETA * D ** -0.5)
    inp['ln1_g'] = 1.0 + nrm((L, D), 0.01)
    inp['ln1_b'] = nrm((L, D), 0.01)
    inp['ln2_g'] = 1.0 + nrm((L, D), 0.01)
    inp['ln2_b'] = nrm((L, D), 0.01)
    inp['peer_wq'] = nrm((L, D, PEER_HEADS * PEER_QDIM), D ** -0.5)
    inp['peer_k1'] = nrm((L, PEER_HEADS, PEER_NKEYS, PEER_QDIM // 2), (PEER_QDIM // 2) ** -0.5)
    inp['peer_k2'] = nrm((L, PEER_HEADS, PEER_NKEYS, PEER_QDIM // 2), (PEER_QDIM // 2) ** -0.5)
    inp['peer_u'] = nrm((L, PEER_EXPERTS, D), D ** -0.5)
    inp['peer_v'] = nrm((L, PEER_EXPERTS, D), DEEPNORM_BETA)
    return inp


def reference(x, c, ctx, c_ctx, w_ada, b_ada, w_in, lam_q1, lam_k1, lam_q2, lam_k2, da_norm_g,
              s5_a_re, s5_a_im, s5_log_step, s5_b_re, s5_b_im, s5_c_re, s5_c_im, s5_d, s5_glu_w, s5_glu_b,
              gla_gate_up, gla_gate_b, gla_norm_g, w_branch, w_out, ln1_g, ln1_b, ln2_g, ln2_b,
              peer_wq, peer_k1, peer_k2, peer_u, peer_v):
    xl, xc = x, ctx
    for l in range(DEPTH):
        need_ctx = l < DEPTH - 1
        lam_init = 0.8 - 0.6 * math.exp(-0.3 * l)
        mod_l = (jax.nn.silu(c) @ w_ada[l] + b_ada[l])[:, None, :]
        mod_c = (jax.nn.silu(c_ctx) @ w_ada[l] + b_ada[l])[None, None, :]
        sh1_l, sc1_l, g1_l, sh2_l, sc2_l, g2_l = jnp.split(mod_l, 6, axis=-1)
        sh1_c, sc1_c, g1_c, sh2_c, sc2_c, g2_c = jnp.split(mod_c, 6, axis=-1)
        mix_c, mix_l = token_mixer(xc * (1.0 + sc1_c) + sh1_c, xl * (1.0 + sc1_l) + sh1_l, w_in[l],
                                   lam_q1[l], lam_k1[l], lam_q2[l], lam_k2[l], da_norm_g[l],
                                   s5_a_re[l], s5_a_im[l], s5_log_step[l], s5_b_re[l], s5_b_im[l],
                                   s5_c_re[l], s5_c_im[l], s5_d[l], s5_glu_w[l], s5_glu_b[l],
                                   gla_gate_up[l], gla_gate_b[l], gla_norm_g[l],
                                   w_branch[l], w_out[l], lam_init, need_ctx)
        xl = layer_norm(DEEPNORM_ALPHA * xl + g1_l * mix_l, ln1_g[l], ln1_b[l])
        ffn_l = peer(xl * (1.0 + sc2_l) + sh2_l, peer_wq[l], peer_k1[l], peer_k2[l], peer_u[l], peer_v[l])
        xl = layer_norm(DEEPNORM_ALPHA * xl + g2_l * ffn_l, ln2_g[l], ln2_b[l])
        if need_ctx:
            xc = layer_norm(DEEPNORM_ALPHA * xc + g1_c * mix_c, ln1_g[l], ln1_b[l])
            ffn_c = peer(xc * (1.0 + sc2_c) + sh2_c, peer_wq[l], peer_k1[l], peer_k2[l], peer_u[l], peer_v[l])
            xc = layer_norm(DEEPNORM_ALPHA * xc + g2_c * ffn_c, ln2_g[l], ln2_b[l])
    return xl
```

```python
import functools
import math

import jax
import jax.numpy as jnp
import numpy as np
from jax import lax
from jax.experimental import pallas as pl
from jax.experimental.pallas import tpu as pltpu

F32 = jnp.float32
BF16 = jnp.bfloat16

D_MODEL = 2048
GRID_W = 64
DA_HEADS = 8
DA_QK_DIM = 64
DA_V_DIM = 128
AX_DIM = DA_QK_DIM // 2
ROPE_BASE = 10000.0
S5_WIDTH = 1024
S5_GROUP = 16
S5_GROUPS = 64
S5_STATE = 64
GLA_HEADS = 4
GLA_DK = 128
GLA_DV = 256
GLA_RANK = 16
GLA_CHUNK = 64
GLA_GATE_NORM = 16.0
PEER_HEADS = 8
PEER_NKEYS = 128
PEER_TOPK = 16
LN_EPS = 1e-5

LANES = 128
ROW_BLOCK = 256
S5_CHUNK = 16
VMEM_LIMIT = 56 * 1024 * 1024

COL_MG = 0
COL_Q = 6144
COL_K = 7168
COL_V = 8192
COL_U = 9216
COL_GQ = 10240
COL_GK = 10752
COL_GV = 11264
COL_R = 12288
N_MAIN = 13312


def _cparams(sem):
    return pltpu.CompilerParams(dimension_semantics=sem, vmem_limit_bytes=VMEM_LIMIT)


def _pick(n, cands):
    for c in cands:
        if n % c == 0:
            return c
    raise ValueError(f"no tile for {n}")


def _gelu(x):
    return 0.5 * x * (1.0 + jnp.tanh(math.sqrt(2.0 / math.pi) * (x + 0.044715 * (x * x * x))))


def _sigmoid(x):
    return 1.0 / (1.0 + jnp.exp(-x))


def _ada_kernel(c_ref, w_ref, b_ref, o_ref):
    c = c_ref[...]
    s = (c * _sigmoid(c)).astype(BF16)
    o_ref[...] = jnp.dot(s, w_ref[...].astype(BF16), preferred_element_type=F32) + b_ref[...]


def _ada_mod(c8, w_ada, b_ada):
    L, D, N = w_ada.shape
    tn = 1536
    return pl.pallas_call(
        _ada_kernel,
        grid=(L, N // tn),
        in_specs=[pl.BlockSpec((8, D), lambda l, j: (0, 0)),
                  pl.BlockSpec((None, D, tn), lambda l, j: (l, 0, j)),
                  pl.BlockSpec((None, 1, tn), lambda l, j: (l, 0, j))],
        out_specs=pl.BlockSpec((None, 8, tn), lambda l, j: (l, 0, j)),
        out_shape=jax.ShapeDtypeStruct((L, 8, N), F32),
        compiler_params=_cparams(("parallel", "parallel")),
        name="ada_mod",
    )(c8, w_ada, b_ada.reshape(L, 1, N))


def _mod_spec(k, nblk, ctx_row):
    return pl.BlockSpec((None, None, 1, D_MODEL),
                        lambda i: (jnp.where(i % nblk == 0, ctx_row, i // nblk), k, 0, 0))


def _modulate_kernel(x_ref, sc_ref, sh_ref, o_ref):
    o_ref[...] = (x_ref[...] * (1.0 + sc_ref[...]) + sh_ref[...]).astype(o_ref.dtype)


def _modulate(x, mod_l, nblk, ctx_row):
    T, D = x.shape
    return pl.pallas_call(
        _modulate_kernel,
        grid=(T // ROW_BLOCK,),
        in_specs=[pl.BlockSpec((ROW_BLOCK, D), lambda i: (i, 0)),
                  _mod_spec(1, nblk, ctx_row), _mod_spec(0, nblk, ctx_row)],
        out_specs=pl.BlockSpec((ROW_BLOCK, D), lambda i: (i, 0)),
        out_shape=jax.ShapeDtypeStruct((T, D), BF16),
        compiler_params=_cparams(("parallel",)),
        name="modulate",
    )(x, mod_l, mod_l)


def _mm_kernel(a_ref, b_ref, o_ref):
    o_ref[...] = jnp.dot(a_ref[...], b_ref[...], preferred_element_type=F32).astype(o_ref.dtype)


def _matmul(a, b, out_dtype, name):
    M, K = a.shape
    N = b.shape[1]
    tm = _pick(M, (1024, 768, 512, 256))
    tn = _pick(N, (1024, 512, 256, 128))
    return pl.pallas_call(
        _mm_kernel,
        grid=(M // tm, N // tn),
        in_specs=[pl.BlockSpec((tm, K), lambda i, j: (i, 0)),
                  pl.BlockSpec((K, tn), lambda i, j: (0, j))],
        out_specs=pl.BlockSpec((tm, tn), lambda i, j: (i, j)),
        out_shape=jax.ShapeDtypeStruct((M, N), out_dtype),
        compiler_params=_cparams(("parallel", "parallel")),
        name=name,
    )(a, b)


def _attn_kernel(lam_ref, g_ref, q_ref, k_ref, v_ref, cos_ref, sa_ref, sb_ref, o_ref, kr_ref,
                 *, lam_init, nkv, tq):
    qi = pl.program_id(2)

    def rope(x, r0, n):
        c = cos_ref[pl.ds(r0, n), :]
        sa = sa_ref[pl.ds(r0, n), :]
        sb = sb_ref[pl.ds(r0, n), :]
        return x * c + pltpu.roll(x, 16, 1) * sa + pltpu.roll(x, LANES - 16, 1) * sb

    @pl.when(qi == 0)
    def _():
        def body(j, carry):
            r0 = pl.multiple_of(j * tq, tq)
            kr_ref[pl.ds(r0, tq), :] = rope(k_ref[pl.ds(r0, tq), :].astype(F32), r0, tq).astype(BF16)
            return carry
        lax.fori_loop(0, nkv, body, 0)

    lv = lam_ref[...]
    lam = (jnp.exp(jnp.sum(lv[0:1] * lv[1:2], axis=1, keepdims=True))
           - jnp.exp(jnp.sum(lv[2:3] * lv[3:4], axis=1, keepdims=True)) + lam_init)

    q = rope(q_ref[...].astype(F32), pl.multiple_of(qi * tq, tq), tq) * (DA_QK_DIM ** -0.5)
    lane = lax.broadcasted_iota(jnp.int32, q.shape, 1)
    qs = (jnp.where(lane < DA_QK_DIM, q, 0.0).astype(BF16), jnp.where(lane >= DA_QK_DIM, q, 0.0).astype(BF16))

    def step(j, carry):
        r0 = pl.multiple_of(j * tq, tq)
        kj = kr_ref[pl.ds(r0, tq), :]
        vj = v_ref[pl.ds(r0, tq), :]
        out = []
        for c in range(2):
            m, l, acc = carry[3 * c:3 * c + 3]
            s = lax.dot_general(qs[c], kj, (((1,), (1,)), ((), ())), preferred_element_type=F32)
            mn = jnp.maximum(m, jnp.max(s, axis=1, keepdims=True))
            p = jnp.exp(s - mn)
            a = jnp.exp(m - mn)
            l = a * l + jnp.sum(p, axis=1, keepdims=True)
            acc = a * acc + jnp.dot(p.astype(BF16), vj, preferred_element_type=F32)
            out += [mn, l, acc]
        return tuple(out)

    m0 = jnp.full((tq, 1), -1e30, F32)
    l0 = jnp.zeros((tq, 1), F32)
    a0 = jnp.zeros((tq, DA_V_DIM), F32)
    nsteps = jnp.where(qi == 0, 1, nkv)
    m_a, l_a, acc_a, m_b, l_b, acc_b = lax.fori_loop(0, nsteps, step, (m0, l0, a0, m0, l0, a0))
    o = acc_a / l_a - lam * (acc_b / l_b)
    ms = jnp.mean(o * o, axis=1, keepdims=True)
    o = o * lax.rsqrt(ms + LN_EPS) * g_ref[...] * (1.0 - lam_init)
    o_ref[...] = o.astype(o_ref.dtype)


def _rope_tables(ctx_len, seq):
    n = np.arange(seq)
    pos = np.stack([n // GRID_W, n % GRID_W], axis=0).astype(np.float32)
    inv = jnp.asarray(ROPE_BASE, F32) ** (-jnp.arange(AX_DIM // 2, dtype=F32) * 2.0 / AX_DIM)
    lane = np.arange(LANES)
    ax = (lane % DA_QK_DIM) // AX_DIM
    half = (lane % AX_DIM) // (AX_DIM // 2)
    j = lane % (AX_DIM // 2)
    ang = jnp.asarray(pos)[ax, :].T * inv[j][None, :]
    cos, sin = jnp.cos(ang), jnp.sin(ang)
    sa = jnp.where(half[None, :] == 1, sin, 0.0)
    sb = jnp.where(half[None, :] == 0, -sin, 0.0)
    pad = lambda t, v: jnp.concatenate([jnp.full((ctx_len, LANES), v, F32), t.astype(F32)], axis=0)
    return pad(cos, 1.0), pad(sa, 0.0), pad(sb, 0.0)


def _diff_attention(p3, lamv, norm_g, tables, lam_init):
    B, S, _ = p3.shape
    tq = ROW_BLOCK
    nq = S // tq
    cq, ck, cv = COL_Q // LANES, COL_K // LANES, COL_V // LANES
    tab_spec = pl.BlockSpec((S, LANES), lambda b, h, i: (0, 0))
    return pl.pallas_call(
        functools.partial(_attn_kernel, lam_init=lam_init, nkv=nq, tq=tq),
        grid=(B, DA_HEADS, nq),
        in_specs=[pl.BlockSpec((8, LANES), lambda b, h, i: (0, 0)),
                  pl.BlockSpec((1, LANES), lambda b, h, i: (0, 0)),
                  pl.BlockSpec((None, tq, LANES), lambda b, h, i: (b, i, cq + h)),
                  pl.BlockSpec((None, S, LANES), lambda b, h, i: (b, 0, ck + h)),
                  pl.BlockSpec((None, S, LANES), lambda b, h, i: (b, 0, cv + h)),
                  tab_spec, tab_spec, tab_spec],
        out_specs=pl.BlockSpec((None, tq, LANES), lambda b, h, i: (b, i, h)),
        out_shape=jax.ShapeDtypeStruct((B, S, DA_HEADS * DA_V_DIM), BF16),
        scratch_shapes=[pltpu.VMEM((S, LANES), BF16)],
        compiler_params=_cparams(("parallel", "parallel", "arbitrary")),
        name="diff_attention",
    )(lamv, norm_g, p3, p3, p3, *tables)


def _s5_params(a_re, a_im, log_step, b_re, b_im, c_re, c_im, d, batch):
    C = S5_CHUNK
    G, P, I = S5_GROUPS, S5_STATE, S5_GROUP
    hp = lax.Precision.HIGHEST
    dt = jnp.exp(log_step.astype(F32))[:, :, None]
    ar, ai = a_re.astype(F32), a_im.astype(F32)
    mag = jnp.exp(ar * dt)
    l_re, l_im = mag * jnp.cos(ai * dt), mag * jnp.sin(ai * dt)
    nr, ni = l_re - 1.0, l_im
    den = ar * ar + ai * ai
    f_re = (nr * ar + ni * ai) / den
    f_im = (ni * ar - nr * ai) / den
    br, bi = b_re.astype(F32), b_im.astype(F32)
    bb_re = f_re[..., None] * br - f_im[..., None] * bi
    bb_im = f_re[..., None] * bi + f_im[..., None] * br
    cr, ci = c_re.astype(F32), c_im.astype(F32)
    kpow = jnp.arange(C + 1, dtype=F32)[None, :, None, None]
    pmag = jnp.exp(kpow * (ar * dt)[:, None])
    pw_re = pmag * jnp.cos(kpow * (ai * dt)[:, None])
    pw_im = pmag * jnp.sin(kpow * (ai * dt)[:, None])
    cp_re = cr[:, None] * pw_re[:, :C, :, None, :] - ci[:, None] * pw_im[:, :C, :, None, :]
    cp_im = cr[:, None] * pw_im[:, :C, :, None, :] + ci[:, None] * pw_re[:, :C, :, None, :]
    kk = (jnp.einsum('dlgop,dgpi->dlgoi', cp_re, bb_re, precision=hp)
          - jnp.einsum('dlgop,dgpi->dlgoi', cp_im, bb_im, precision=hp))
    s_idx = np.arange(C)[:, None]
    j_idx = np.arange(C)[None, :]
    lag_f = np.clip(j_idx - s_idx, 0, C - 1)
    lag_b = np.clip(s_idx - j_idx, 0, C - 1)
    mf = jnp.where((j_idx >= s_idx)[:, :, None, None, None], kk[0][lag_f], 0.0)
    mb = jnp.where((s_idx >= j_idx)[:, :, None, None, None], kk[1][lag_b], 0.0)
    m = (mf + mb).transpose(2, 0, 4, 1, 3).reshape(G, C * I, C * I)
    def wmat(d, idx):
        pr, pi = pw_re[d][idx][..., None], pw_im[d][idx][..., None]
        w_re = pr * bb_re[d][None] - pi * bb_im[d][None]
        w_im = pr * bb_im[d][None] + pi * bb_re[d][None]
        t = lambda w: w.transpose(1, 0, 3, 2).reshape(G, C * I, P)
        return jnp.concatenate([t(w_re), t(w_im)], axis=-1)
    w = jnp.concatenate([wmat(0, C - 1 - np.arange(C)), wmat(1, np.arange(C))], axis=-1)
    def vmat(d, idx):
        pr, pi = pw_re[d][idx][:, :, None, :], pw_im[d][idx][:, :, None, :]
        v_re = cr[d][None] * pr - ci[d][None] * pi
        v_im = cr[d][None] * pi + ci[d][None] * pr
        t = lambda v: v.transpose(1, 3, 0, 2).reshape(G, P, C * I)
        return jnp.concatenate([t(v_re), -t(v_im)], axis=1)
    v = jnp.concatenate([vmat(0, 1 + np.arange(C)), vmat(1, C - np.arange(C))], axis=1)
    lc_re, lc_im = pw_re[:, C], pw_im[:, C]
    def rows(t):
        return jnp.repeat(t, batch, axis=0)
    ca = jnp.stack([rows(jnp.concatenate([lc_re[k], lc_re[k]], -1)) for k in range(2)])
    cb = jnp.stack([rows(jnp.concatenate([-lc_im[k], lc_im[k]], -1)) for k in range(2)])
    dflat = jnp.tile(d.astype(F32).reshape(G, 1, I), (1, C, 1)).reshape(G, 1, C * I)
    return m.astype(BF16), w.astype(BF16), v.astype(BF16), ca, cb, dflat


S5_GBLK = 8


def _s5_sum_kernel(u_ref, w_ref, o_ref):
    for g in range(S5_GBLK):
        o_ref[g] = jnp.dot(u_ref[g], w_ref[g], preferred_element_type=F32)


def _s5_summaries(ug, w):
    G, R, K = ug.shape
    spec = lambda n: pl.BlockSpec((S5_GBLK, n, K), lambda i: (i, 0, 0))
    return pl.pallas_call(
        _s5_sum_kernel, grid=(G // S5_GBLK,),
        in_specs=[spec(R), spec(K)], out_specs=spec(R),
        out_shape=jax.ShapeDtypeStruct((G, R, K), F32),
        compiler_params=_cparams(("parallel",)), name="s5_summaries",
    )(ug, w)


def _s5_scan_kernel(sf_ref, sb_ref, ca_ref, cb_ref, xf_out, xb_out, xf_ref, xb_ref, *, nch):
    i = pl.program_id(0)

    @pl.when(i == 0)
    def _():
        xf_ref[...] = jnp.zeros_like(xf_ref)
        xb_ref[...] = jnp.zeros_like(xb_ref)

    caf, cbf, cab, cbb = ca_ref[0], cb_ref[0], ca_ref[1], cb_ref[1]
    xf = xf_ref[...]
    xb = xb_ref[...]
    for j in range(nch):
        xf_out[j] = xf.astype(BF16)
        xf = xf * caf + pltpu.roll(xf, S5_STATE, 1) * cbf + sf_ref[j]
        jb = nch - 1 - j
        xb_out[jb] = xb.astype(BF16)
        xb = xb * cab + pltpu.roll(xb, S5_STATE, 1) * cbb + sb_ref[jb]
    xf_ref[...] = xf
    xb_ref[...] = xb


def _s5_scan(st, ca, cb):
    NC, R, _ = st.shape
    nch = ROW_BLOCK // S5_CHUNK
    nblk = NC // nch
    bwd = lambda i: jnp.where(i == 0, 0, nblk - i)
    P2 = 2 * S5_STATE
    return pl.pallas_call(
        functools.partial(_s5_scan_kernel, nch=nch), grid=(nblk,),
        in_specs=[pl.BlockSpec((nch, R, P2), lambda i: (i, 0, 0)),
                  pl.BlockSpec((nch, R, P2), lambda i: (bwd(i), 0, 1)),
                  pl.BlockSpec((2, R, P2), lambda i: (0, 0, 0)),
                  pl.BlockSpec((2, R, P2), lambda i: (0, 0, 0))],
        out_specs=[pl.BlockSpec((nch, R, P2), lambda i: (i, 0, 0)),
                   pl.BlockSpec((nch, R, P2), lambda i: (bwd(i), 0, 0))],
        out_shape=[jax.ShapeDtypeStruct((NC, R, P2), BF16)] * 2,
        scratch_shapes=[pltpu.VMEM((R, P2), F32), pltpu.VMEM((R, P2), F32)],
        compiler_params=_cparams(("arbitrary",)), name="s5_scan",
    )(st, st, ca, cb)


def _s5_out_kernel(u_ref, xp_ref, m_ref, v_ref, d_ref, o_ref):
    for g in range(S5_GBLK):
        u = u_ref[g]
        y = (jnp.dot(u, m_ref[g], preferred_element_type=F32)
             + jnp.dot(xp_ref[g], v_ref[g], preferred_element_type=F32)
             + u.astype(F32) * d_ref[g])
        o_ref[g] = _gelu(y).astype(o_ref.dtype)


def _s5_output(ug, xp, m, v, dflat):
    G, R, K = ug.shape
    spec = lambda n: pl.BlockSpec((S5_GBLK, n, K), lambda i: (i, 0, 0))
    return pl.pallas_call(
        _s5_out_kernel, grid=(G // S5_GBLK,),
        in_specs=[spec(R), spec(R), spec(K), spec(K), spec(1)], out_specs=spec(R),
        out_shape=jax.ShapeDtypeStruct((G, R, K), BF16),
        compiler_params=_cparams(("parallel",)), name="s5_output",
    )(ug, xp, m, v, dflat)


def _glu_kernel(z_ref, w_ref, b_ref, o_ref):
    z = z_ref[...]
    t = jnp.dot(z, w_ref[...], preferred_element_type=F32) + b_ref[...]
    o_ref[...] = (z.astype(F32) * _sigmoid(t)).astype(o_ref.dtype)


def _glu(z, w, b):
    T, N = z.shape
    tm = _pick(T, (1024, 768, 512, 256))
    return pl.pallas_call(
        _glu_kernel, grid=(T // tm,),
        in_specs=[pl.BlockSpec((tm, N), lambda i: (i, 0)), pl.BlockSpec((N, N), lambda i: (0, 0)),
                  pl.BlockSpec((1, N), lambda i: (0, 0))],
        out_specs=pl.BlockSpec((tm, N), lambda i: (i, 0)),
        out_shape=jax.ShapeDtypeStruct((T, N), BF16),
        compiler_params=_cparams(("parallel",)), name="s5_glu",
    )(z, w, b)


def _s5_mixer(p3, prm, glu_w, glu_b):
    B, S, _ = p3.shape
    m, w, v, ca, cb, dflat = prm
    C, G, I = S5_CHUNK, S5_GROUPS, S5_GROUP
    nc = S // C
    u = p3[:, :, COL_U:COL_U + S5_WIDTH]
    ug = u.reshape(B, nc, C, G, I).transpose(3, 0, 1, 2, 4).reshape(G, B * nc, C * I)
    s = _s5_summaries(ug, w)
    st = s.reshape(G, B, nc, 4 * S5_STATE).transpose(2, 0, 1, 3).reshape(nc, G * B, 4 * S5_STATE)
    xf, xb = _s5_scan(st, ca, cb)
    xp = jnp.concatenate([xf, xb], axis=-1).reshape(nc, G, B, 4 * S5_STATE)
    xp = xp.transpose(1, 2, 0, 3).reshape(G, B * nc, 4 * S5_STATE)
    z = _s5_output(ug, xp, m, v, dflat)
    z = z.reshape(G, B, nc, C, I).transpose(1, 2, 3, 0, 4).reshape(B * S, S5_WIDTH)
    return _glu(z, glu_w.astype(BF16), glu_b.astype(F32).reshape(1, S5_WIDTH))


def _gla_kernel(qf, kf, vf, gf, qb, kb, vb, gb, gup_ref, gbias_ref, of_ref, ob_ref, sf_ref, sb_ref, *, batch):
    i = pl.program_id(0)
    C = GLA_CHUNK

    @pl.when(i == 0)
    def _():
        sf_ref[...] = jnp.zeros_like(sf_ref)
        sb_ref[...] = jnp.zeros_like(sb_ref)

    row = lax.broadcasted_iota(jnp.int32, (C, C), 0)
    col = lax.broadcasted_iota(jnp.int32, (C, C), 1)
    dn_t = (((1,), (1,)), ((), ()))

    def one(b, d, q_ref, k_ref, v_ref, g_ref, o_ref, s_ref):
        tri = (row >= col) if d == 0 else (row <= col)
        trib = jnp.where(tri, 1.0, 0.0).astype(BF16)
        logits = jnp.dot(g_ref[b], gup_ref[d], preferred_element_type=F32) + gbias_ref[d]
        la = (jnp.minimum(logits, 0.0) - jnp.log(1.0 + jnp.exp(-jnp.abs(logits)))) * (1.0 / GLA_GATE_NORM)
        hi = la.astype(BF16)
        r1 = la - hi.astype(F32)
        mid = r1.astype(BF16)
        lo = (r1 - mid.astype(F32)).astype(BF16)
        bcum = (jnp.dot(trib, hi, preferred_element_type=F32) + jnp.dot(trib, mid, preferred_element_type=F32)
                + jnp.dot(trib, lo, preferred_element_type=F32))
        total = bcum[C - 1:C] if d == 0 else bcum[0:1]
        q = q_ref[b].astype(F32) * (GLA_DK ** -0.5)
        k = k_ref[b].astype(F32)
        qt = (q * jnp.exp(bcum)).astype(BF16)
        kt = (k * jnp.exp(-bcum)).astype(BF16)
        ks = (k * jnp.exp(total - bcum)).astype(BF16)
        et = jnp.exp(total)
        v = v_ref[b]
        for h in range(GLA_HEADS):
            sl = slice(h * GLA_DK, (h + 1) * GLA_DK)
            vs = v[:, h * GLA_DV:(h + 1) * GLA_DV]
            attn = lax.dot_general(qt[:, sl], kt[:, sl], dn_t, preferred_element_type=F32)
            attn = jnp.where(tri, attn, 0.0).astype(BF16)
            st = s_ref[b * GLA_HEADS + h]
            o = (jnp.dot(attn, vs, preferred_element_type=F32)
                 + lax.dot_general(qt[:, sl], st.astype(BF16), dn_t, preferred_element_type=F32))
            o_ref[b, :, h * GLA_DV:(h + 1) * GLA_DV] = o
            upd = lax.dot_general(vs, ks[:, sl], (((0,), (0,)), ((), ())), preferred_element_type=F32)
            s_ref[b * GLA_HEADS + h] = st * et[:, sl] + upd

    def body(b, carry):
        one(b, 0, qf, kf, vf, gf, of_ref, sf_ref)
        one(b, 1, qb, kb, vb, gb, ob_ref, sb_ref)
        return carry
    lax.fori_loop(0, batch, body, 0)


def _gla(p3, pg3, gup, gbias):
    B, S, _ = p3.shape
    C = GLA_CHUNK
    nch = S // C
    nctx = ROW_BLOCK // C
    bwd = lambda i: jnp.where(i < nctx, nctx - 1 - i, nch + nctx - 1 - i)
    wq, wv = GLA_HEADS * GLA_DK, GLA_HEADS * GLA_DV
    def specs(f):
        return [pl.BlockSpec((B, C, wq), lambda i: (0, f(i), COL_GQ // wq)),
                pl.BlockSpec((B, C, wq), lambda i: (0, f(i), COL_GK // wq)),
                pl.BlockSpec((B, C, wv), lambda i: (0, f(i), COL_GV // wv)),
                pl.BlockSpec((B, C, LANES), lambda i: (0, f(i), 0))]
    fwd = lambda i: i
    return pl.pallas_call(
        functools.partial(_gla_kernel, batch=B), grid=(nch,),
        in_specs=specs(fwd) + specs(bwd) + [pl.BlockSpec((2, LANES, wq), lambda i: (0, 0, 0)),
                                            pl.BlockSpec((2, 1, wq), lambda i: (0, 0, 0))],
        out_specs=[pl.BlockSpec((B, C, wv), lambda i: (0, i, 0)),
                   pl.BlockSpec((B, C, wv), lambda i: (0, bwd(i), 0))],
        out_shape=[jax.ShapeDtypeStruct((B, S, wv), F32)] * 2,
        scratch_shapes=[pltpu.VMEM((B * GLA_HEADS, GLA_DV, GLA_DK), F32)] * 2,
        compiler_params=_cparams(("arbitrary",)), name="gla",
    )(p3, p3, p3, pg3, p3, p3, p3, pg3, gup, gbias)


def _merge_kernel(oa_ref, ob_ref, of_ref, obk_ref, r_ref, g0_ref, g1_ref, g2_ref, gn_ref, wb_ref, y_ref):
    o = of_ref[...] + obk_ref[...]
    gn = gn_ref[...]
    parts = []
    for h in range(GLA_HEADS):
        oh = o[:, h * GLA_DV:(h + 1) * GLA_DV]
        ms = jnp.mean(oh * oh, axis=1, keepdims=True)
        parts.append(oh * lax.rsqrt(ms + LN_EPS) * gn)
    r = r_ref[...].astype(F32)
    oc = (jnp.concatenate(parts, axis=1) * (r * _sigmoid(r))).astype(BF16)
    y = (_sigmoid(g0_ref[...].astype(F32)) * jnp.dot(oa_ref[...], wb_ref[0], preferred_element_type=F32)
         + _sigmoid(g1_ref[...].astype(F32)) * jnp.dot(ob_ref[...], wb_ref[1], preferred_element_type=F32)
         + _sigmoid(g2_ref[...].astype(F32)) * jnp.dot(oc, wb_ref[2], preferred_element_type=F32))
    y_ref[...] = y.astype(y_ref.dtype)


def _merge(oa, ob, of, obk, p2, gn, wb):
    T = oa.shape[0]
    D = D_MODEL
    W = 1024
    tm = ROW_BLOCK
    row = lambda n: pl.BlockSpec((tm, n), lambda i: (i, 0))
    return pl.pallas_call(
        _merge_kernel, grid=(T // tm,),
        in_specs=[row(W), row(W), row(W), row(W),
                  pl.BlockSpec((tm, W), lambda i: (i, COL_R // W)),
                  pl.BlockSpec((tm, D), lambda i: (i, 0)),
                  pl.BlockSpec((tm, D), lambda i: (i, 1)),
                  pl.BlockSpec((tm, D), lambda i: (i, 2)),
                  pl.BlockSpec((1, GLA_DV), lambda i: (0, 0)),
                  pl.BlockSpec((3, W, D), lambda i: (0, 0, 0))],
        out_specs=row(D),
        out_shape=jax.ShapeDtypeStruct((T, D), BF16),
        compiler_params=_cparams(("parallel",)), name="merge",
    )(oa, ob, of, obk, p2, p2, p2, p2, gn, wb)


def _layer_norm(z, g, b):
    mu = jnp.mean(z, axis=1, keepdims=True)
    zc = z - mu
    var = jnp.mean(zc * zc, axis=1, keepdims=True)
    return zc * lax.rsqrt(var + LN_EPS) * g + b


def _outln_kernel(y_ref, w_ref, x_ref, g1_ref, sc_ref, sh_ref, lng_ref, lnb_ref, x_out, h_out, *, alpha):
    mix = jnp.dot(y_ref[...], w_ref[...], preferred_element_type=F32)
    xn = _layer_norm(alpha * x_ref[...] + g1_ref[...] * mix, lng_ref[...], lnb_ref[...])
    x_out[...] = xn
    h_out[...] = (xn * (1.0 + sc_ref[...]) + sh_ref[...]).astype(h_out.dtype)


def _out_ln(y, w_out, x, mod_l, lng, lnb, nblk, ctx_row, alpha):
    T, D = x.shape
    tm = ROW_BLOCK
    row = pl.BlockSpec((tm, D), lambda i: (i, 0))
    vec = pl.BlockSpec((1, D), lambda i: (0, 0))
    return pl.pallas_call(
        functools.partial(_outln_kernel, alpha=alpha), grid=(T // tm,),
        in_specs=[row, pl.BlockSpec((D, D), lambda i: (0, 0)), row,
                  _mod_spec(2, nblk, ctx_row), _mod_spec(4, nblk, ctx_row), _mod_spec(3, nblk, ctx_row), vec, vec],
        out_specs=[row, row],
        out_shape=[jax.ShapeDtypeStruct((T, D), F32), jax.ShapeDtypeStruct((T, D), BF16)],
        compiler_params=_cparams(("parallel",)), name="out_proj_ln",
    )(y, w_out, x, mod_l, mod_l, mod_l, lng, lnb)


def _ln2_kernel(x_ref, f_ref, g2_ref, lng_ref, lnb_ref, x_out, *, alpha):
    x_out[...] = _layer_norm(alpha * x_ref[...] + g2_ref[...] * f_ref[...], lng_ref[...], lnb_ref[...])


def _ln2(x, ffn, mod_l, lng, lnb, nblk, ctx_row, alpha):
    T, D = x.shape
    tm = ROW_BLOCK
    row = pl.BlockSpec((tm, D), lambda i: (i, 0))
    vec = pl.BlockSpec((1, D), lambda i: (0, 0))
    return pl.pallas_call(
        functools.partial(_ln2_kernel, alpha=alpha), grid=(T // tm,),
        in_specs=[row, row, _mod_spec(5, nblk, ctx_row), vec, vec],
        out_specs=row,
        out_shape=jax.ShapeDtypeStruct((T, D), F32),
        compiler_params=_cparams(("parallel",)), name="ffn_ln",
    )(x, ffn, mod_l, lng, lnb)


def _peer_scores_kernel(h_ref, wq_ref, k1_ref, k2_ref, s_ref):
    qt = jnp.dot(wq_ref[...], h_ref[...], preferred_element_type=F32).astype(BF16)
    n = PEER_NKEYS
    for h in range(PEER_HEADS):
        r = 2 * n * h
        s_ref[r:r + n, :] = jnp.dot(k1_ref[h], qt[r:r + n, :], preferred_element_type=F32)
        s_ref[r + n:r + 2 * n, :] = jnp.dot(k2_ref[h], qt[r + n:r + 2 * n, :], preferred_element_type=F32)


def _peer_scores(ht, wqt, k1, k2):
    D, T = ht.shape
    tt = _pick(T, (512, 256))
    Q = wqt.shape[0]
    n = PEER_NKEYS
    return pl.pallas_call(
        _peer_scores_kernel, grid=(T // tt,),
        in_specs=[pl.BlockSpec((D, tt), lambda i: (0, i)), pl.BlockSpec((Q, D), lambda i: (0, 0)),
                  pl.BlockSpec((PEER_HEADS, n, n), lambda i: (0, 0, 0)),
                  pl.BlockSpec((PEER_HEADS, n, n), lambda i: (0, 0, 0))],
        out_specs=pl.BlockSpec((Q, tt), lambda i: (0, i)),
        out_shape=jax.ShapeDtypeStruct((Q, T), F32),
        compiler_params=_cparams(("parallel",)), name="peer_scores",
    )(ht, wqt, k1, k2)


def _peer_select_kernel(s_ref, nt_ref, e1_ref, s2_ref, e2_ref):
    n, K = PEER_NKEYS, PEER_TOPK
    neg = -jnp.inf

    def top_values(s, count):
        vals = []
        for _ in range(count):
            m = jnp.max(s, axis=0, keepdims=True)
            vals.append(m)
            s = jnp.where(s == m, neg, s)
        return vals

    def body(h, carry):
        r1 = pl.multiple_of(h * 2 * n, 2 * n)
        r2 = pl.multiple_of(h * 2 * n + n, n)
        ro = pl.multiple_of(h * n, n)
        s1 = s_ref[pl.ds(r1, n), :]
        s2 = s_ref[pl.ds(r2, n), :]
        v1 = top_values(s1, K + 1)
        v2 = top_values(s2, K + 1)
        v2m = jnp.concatenate(v2[:K], axis=0)
        cand = jnp.concatenate([v1[a] + v2m for a in range(K)], axis=0)
        tops = top_values(cand, K + 1)
        nxt = jnp.maximum(tops[K], jnp.maximum(v1[K] + v2[0], v1[0] + v2[K]))
        tau = 0.5 * (tops[K - 1] + nxt)
        mtop = v1[0] + v2[0]
        z = jnp.sum(jnp.where(cand >= tau, jnp.exp(cand - mtop), 0.0), axis=0, keepdims=True)
        nt_ref[pl.ds(ro, n), :] = tau - s1
        e1_ref[pl.ds(ro, n), :] = jnp.exp(s1 - v1[0]) / z
        s2_ref[pl.ds(ro, n), :] = s2
        e2_ref[pl.ds(ro, n), :] = jnp.exp(s2 - v2[0])
        return carry
    lax.fori_loop(0, PEER_HEADS, body, 0)


def _peer_select(st):
    Q, T = st.shape
    tt = _pick(T, (256, 128))
    R = PEER_HEADS * PEER_NKEYS
    out = pl.BlockSpec((R, tt), lambda i: (0, i))
    return pl.pallas_call(
        _peer_select_kernel, grid=(T // tt,),
        in_specs=[pl.BlockSpec((Q, tt), lambda i: (0, i))],
        out_specs=[out] * 4,
        out_shape=[jax.ShapeDtypeStruct((R, T), F32)] * 4,
        compiler_params=_cparams(("parallel",)), name="peer_select",
    )(st)


def _peer_main_kernel(h_ref, u_ref, vt_ref, nt_ref, e1_ref, s2_ref, e2_ref, o_ref, *, n_i1):
    e = pl.program_id(1)
    n = PEER_NKEYS

    @pl.when(e == 0)
    def _():
        o_ref[...] = jnp.zeros_like(o_ref)

    act = jnp.dot(u_ref[...], h_ref[...], preferred_element_type=F32)
    gates = []
    for a in range(n_i1):
        i1 = e * n_i1 + a
        acc = None
        for h in range(PEER_HEADS):
            ntr = nt_ref[pl.ds(h * n + i1, 1), :]
            e1r = e1_ref[pl.ds(h * n + i1, 1), :]
            g = jnp.where(s2_ref[h * n:(h + 1) * n, :] >= ntr, e1r * e2_ref[h * n:(h + 1) * n, :], 0.0)
            acc = g if acc is None else acc + g
        gates.append(acc)
    gate = jnp.concatenate(gates, axis=0)
    p = (_gelu(act) * gate).astype(BF16)
    o_ref[...] += jnp.dot(vt_ref[...], p, preferred_element_type=F32)


def _peer_main(ht, ub, vtb, nt, e1, s2, e2):
    D, T = ht.shape
    E = ub.shape[0]
    tb = _pick(T, (512, 256))
    te = 512
    R = PEER_HEADS * PEER_NKEYS
    sel = pl.BlockSpec((R, tb), lambda i, e: (0, i))
    return pl.pallas_call(
        functools.partial(_peer_main_kernel, n_i1=te // PEER_NKEYS), grid=(T // tb, E // te),
        in_specs=[pl.BlockSpec((D, tb), lambda i, e: (0, i)),
                  pl.BlockSpec((te, D), lambda i, e: (e, 0)),
                  pl.BlockSpec((D, te), lambda i, e: (0, e)),
                  sel, sel, sel, sel],
        out_specs=pl.BlockSpec((D, tb), lambda i, e: (0, i)),
        out_shape=jax.ShapeDtypeStruct((D, T), F32),
        compiler_params=_cparams(("parallel", "arbitrary")), name="peer_main",
    )(ht, ub, vtb, nt, e1, s2, e2)


def _peer(h2, wq, k1, k2, u_tab, v_tab):
    ht = h2.T
    st = _peer_scores(ht, wq.T.astype(BF16), k1.astype(BF16), k2.astype(BF16))
    nt, e1, s2, e2 = _peer_select(st)
    out_t = _peer_main(ht, u_tab.astype(BF16), v_tab.T.astype(BF16), nt, e1, s2, e2)
    return out_t.T


def kernel(x, c, ctx, c_ctx, w_ada, b_ada, w_in, lam_q1, lam_k1, lam_q2, lam_k2, da_norm_g, s5_a_re, s5_a_im, s5_log_step, s5_b_re, s5_b_im, s5_c_re, s5_c_im, s5_d, s5_glu_w, s5_glu_b, gla_gate_up, gla_gate_b, gla_norm_g, w_branch, w_out, ln1_g, ln1_b, ln2_g, ln2_b, peer_wq, peer_k1, peer_k2, peer_u, peer_v):
    B, SEQ, D = x.shape
    CTX = ctx.shape[1]
    L = w_ada.shape[0]
    assert CTX == ROW_BLOCK and SEQ % ROW_BLOCK == 0 and D == D_MODEL and B < 8
    S = CTX + SEQ
    T = B * S
    nblk = S // ROW_BLOCK
    ctx_row = B
    alpha = (2 * L) ** 0.25

    c8 = jnp.zeros((8, D), F32).at[:B].set(c.astype(F32)).at[B].set(c_ctx.astype(F32))
    mod = _ada_mod(c8, w_ada, b_ada).reshape(L, 8, 6, 1, D)
    tables = _rope_tables(CTX, SEQ)
    xs = jnp.concatenate([ctx, x], axis=1).reshape(T, D).astype(F32)

    sizes = np.cumsum([0, 1024, 1024, 1024, 1024, 512, 512, 1024, 16, 16, 1024, 6144])
    seg = lambda w, k: w[:, sizes[k]:sizes[k + 1]]

    for l in range(L):
        lam_init = 0.8 - 0.6 * math.exp(-0.3 * l)
        mod_l = mod[l]
        wl = w_in[l]
        w_main = jnp.concatenate([seg(wl, 10), seg(wl, 0), seg(wl, 1), seg(wl, 2), seg(wl, 3), seg(wl, 4),
                                  seg(wl, 5), seg(wl, 6), seg(wl, 9)], axis=1).astype(BF16)
        w_gate = jnp.concatenate([seg(wl, 7), seg(wl, 8), jnp.zeros((D, LANES - 2 * GLA_RANK), wl.dtype)],
                                 axis=1).astype(BF16)
        h = _modulate(xs, mod_l, nblk, ctx_row)
        p2 = _matmul(h, w_main, BF16, "in_proj")
        pg = _matmul(h, w_gate, BF16, "in_proj_gates")
        p3 = p2.reshape(B, S, N_MAIN)

        lamv = jnp.zeros((8, LANES), F32)
        for r, t in enumerate((lam_q1[l], lam_k1[l], lam_q2[l], lam_k2[l])):
            lamv = lamv.at[r, :DA_QK_DIM].set(t.astype(F32))
        oa = _diff_attention(p3, lamv, da_norm_g[l].astype(F32).reshape(1, DA_V_DIM), tables, lam_init)

        prm = _s5_params(s5_a_re[l], s5_a_im[l], s5_log_step[l], s5_b_re[l], s5_b_im[l],
                         s5_c_re[l], s5_c_im[l], s5_d[l], B)
        ob = _s5_mixer(p3, prm, s5_glu_w[l], s5_glu_b[l])

        gup = jnp.zeros((2, LANES, GLA_HEADS * GLA_DK), F32)
        gup = gup.at[0, :GLA_RANK].set(gla_gate_up[l, 0]).at[1, GLA_RANK:2 * GLA_RANK].set(gla_gate_up[l, 1])
        gbias = gla_gate_b[l].astype(F32).reshape(2, 1, GLA_HEADS * GLA_DK)
        of, obk = _gla(p3, pg.reshape(B, S, LANES), gup.astype(BF16), gbias)

        y = _merge(oa.reshape(T, -1), ob, of.reshape(T, -1), obk.reshape(T, -1), p2,
                   gla_norm_g[l].astype(F32).reshape(1, GLA_DV), w_branch[l].astype(BF16))
        x1, h2 = _out_ln(y, w_out[l].astype(BF16), xs, mod_l, ln1_g[l].reshape(1, D), ln1_b[l].reshape(1, D),
                         nblk, ctx_row, alpha)
        ffn = _peer(h2, peer_wq[l], peer_k1[l], peer_k2[l], peer_u[l], peer_v[l])
        xs = _ln2(x1, ffn, mod_l, ln2_g[l].reshape(1, D), ln2_b[l].reshape(1, D), nblk, ctx_row, alpha)

    return xs.reshape(B, S, D)[:, CTX:, :]
```

```python
import functools
import math

import jax
import jax.numpy as jnp
import numpy as np
from jax import lax
from jax.experimental import pallas as pl
from jax.experimental.pallas import tpu as pltpu

F32 = jnp.float32
BF16 = jnp.bfloat16

D_MODEL = 2048
GRID_W = 64
DA_HEADS = 8
DA_QK_DIM = 64
DA_V_DIM = 128
AX_DIM = DA_QK_DIM // 2
ROPE_BASE = 10000.0
S5_WIDTH = 1024
S5_GROUP = 16
S5_GROUPS = 64
S5_STATE = 64
GLA_HEADS = 4
GLA_DK = 128
GLA_DV = 256
GLA_RANK = 16
GLA_CHUNK = 64
GLA_GATE_NORM = 16.0
PEER_HEADS = 8
PEER_NKEYS = 128
PEER_TOPK = 16
LN_EPS = 1e-5

LANES = 128
ROW_BLOCK = 256
S5_CHUNK = 16
VMEM_LIMIT = 56 * 1024 * 1024

COL_MG = 0
COL_Q = 6144
COL_K = 7168
COL_V = 8192
COL_U = 9216
COL_GQ = 10240
COL_GK = 10752
COL_GV = 11264
COL_R = 12288
N_MAIN = 13312


def _cparams(sem):
    return pltpu.CompilerParams(dimension_semantics=sem, vmem_limit_bytes=VMEM_LIMIT)


def _pick(n, cands):
    for c in cands:
        if n % c == 0:
            return c
    raise ValueError(f"no tile for {n}")


def _gelu(x):
    return 0.5 * x * (1.0 + jnp.tanh(math.sqrt(2.0 / math.pi) * (x + 0.044715 * (x * x * x))))


def _sigmoid(x):
    return 1.0 / (1.0 + jnp.exp(-x))


def _ada_kernel(c_ref, w_ref, b_ref, o_ref):
    c = c_ref[...]
    s = (c * _sigmoid(c)).astype(BF16)
    o_ref[...] = jnp.dot(s, w_ref[...].astype(BF16), preferred_element_type=F32) + b_ref[...]


def _ada_mod(c8, w_ada, b_ada):
    L, D, N = w_ada.shape
    tn = 1536
    return pl.pallas_call(
        _ada_kernel,
        grid=(L, N // tn),
        in_specs=[pl.BlockSpec((8, D), lambda l, j: (0, 0)),
                  pl.BlockSpec((None, D, tn), lambda l, j: (l, 0, j)),
                  pl.BlockSpec((None, 1, tn), lambda l, j: (l, 0, j))],
        out_specs=pl.BlockSpec((None, 8, tn), lambda l, j: (l, 0, j)),
        out_shape=jax.ShapeDtypeStruct((L, 8, N), F32),
        compiler_params=_cparams(("parallel", "parallel")),
        name="ada_mod",
    )(c8, w_ada, b_ada.reshape(L, 1, N))


def _mod_spec(k, nblk, ctx_row):
    return pl.BlockSpec((None, None, 1, D_MODEL),
                        lambda i: (jnp.where(i % nblk == 0, ctx_row, i // nblk), k, 0, 0))


def _modulate_kernel(x_ref, sc_ref, sh_ref, o_ref):
    o_ref[...] = (x_ref[...] * (1.0 + sc_ref[...]) + sh_ref[...]).astype(o_ref.dtype)


def _modulate(x, mod_l, nblk, ctx_row):
    T, D = x.shape
    return pl.pallas_call(
        _modulate_kernel,
        grid=(T // ROW_BLOCK,),
        in_specs=[pl.BlockSpec((ROW_BLOCK, D), lambda i: (i, 0)),
                  _mod_spec(1, nblk, ctx_row), _mod_spec(0, nblk, ctx_row)],
        out_specs=pl.BlockSpec((ROW_BLOCK, D), lambda i: (i, 0)),
        out_shape=jax.ShapeDtypeStruct((T, D), BF16),
        compiler_params=_cparams(("parallel",)),
        name="modulate",
    )(x, mod_l, mod_l)


def _mm_kernel(a_ref, b_ref, o_ref):
    o_ref[...] = jnp.dot(a_ref[...], b_ref[...], preferred_element_type=F32).astype(o_ref.dtype)


def _matmul(a, b, out_dtype, name):
    M, K = a.shape
    N = b.shape[1]
    tm = _pick(M, (1024, 768, 512, 256))
    tn = _pick(N, (1024, 512, 256, 128))
    return pl.pallas_call(
        _mm_kernel,
        grid=(M // tm, N // tn),
        in_specs=[pl.BlockSpec((tm, K), lambda i, j: (i, 0)),
                  pl.BlockSpec((K, tn), lambda i, j: (0, j))],
        out_specs=pl.BlockSpec((tm, tn), lambda i, j: (i, j)),
        out_shape=jax.ShapeDtypeStruct((M, N), out_dtype),
        compiler_params=_cparams(("parallel", "parallel")),
        name=name,
    )(a, b)


def _attn_kernel(lam_ref, g_ref, q_ref, k_ref, v_ref, cos_ref, sa_ref, sb_ref, o_ref, kr_ref, ve_ref,
                 *, lam_init, n_ctx, tk, n_lat, tq):
    qi = pl.program_id(2)
    n_rows = n_ctx + n_lat * tk

    def rope(x, r0, n):
        c = cos_ref[pl.ds(r0, n), :]
        sa = sa_ref[pl.ds(r0, n), :]
        sb = sb_ref[pl.ds(r0, n), :]
        return x * c + pltpu.roll(x, 16, 1) * sa + pltpu.roll(x, LANES - 16, 1) * sb

    @pl.when(qi == 0)
    def _():
        def body(j, carry):
            r0 = pl.multiple_of(j * tq, tq)
            kr_ref[pl.ds(r0, tq), :] = rope(k_ref[pl.ds(r0, tq), :].astype(F32), r0, tq).astype(BF16)
            ve_ref[pl.ds(r0, tq), 0:DA_V_DIM] = v_ref[pl.ds(r0, tq), :]
            ve_ref[pl.ds(r0, tq), DA_V_DIM:2 * DA_V_DIM] = jnp.ones((tq, DA_V_DIM), BF16)
            return carry
        lax.fori_loop(0, n_rows // tq, body, 0)

    lv = lam_ref[...]
    lam = (jnp.exp(jnp.sum(lv[0:1] * lv[1:2], axis=1, keepdims=True))
           - jnp.exp(jnp.sum(lv[2:3] * lv[3:4], axis=1, keepdims=True)) + lam_init)

    q = rope(q_ref[...].astype(F32), pl.multiple_of(qi * tq, tq), tq) * (DA_QK_DIM ** -0.5 * math.log2(math.e))
    lane = lax.broadcasted_iota(jnp.int32, q.shape, 1)
    qs = (jnp.where(lane < DA_QK_DIM, q, 0.0).astype(BF16), jnp.where(lane >= DA_QK_DIM, q, 0.0).astype(BF16))

    def chunk(carry, r0, n):
        kj = kr_ref[r0:r0 + n, :]
        vj = ve_ref[r0:r0 + n, :]
        out = []
        for c in range(2):
            s = lax.dot_general(qs[c], kj, (((1,), (1,)), ((), ())), preferred_element_type=F32)
            mx = jnp.max(s, axis=1, keepdims=True)
            if carry is None:
                mn = mx
                acc = jnp.dot(jnp.exp2(s - mn).astype(BF16), vj, preferred_element_type=F32)
            else:
                m, acc = carry[c]
                mn = jnp.maximum(m, mx)
                acc = jnp.exp2(m - mn) * acc + jnp.dot(jnp.exp2(s - mn).astype(BF16), vj,
                                                      preferred_element_type=F32)
            out.append((mn, acc))
        return out

    def finish(carry):
        (_, acc_a), (_, acc_b) = carry
        o = (acc_a[:, :DA_V_DIM] / acc_a[:, DA_V_DIM:] - lam * (acc_b[:, :DA_V_DIM] / acc_b[:, DA_V_DIM:]))
        ms = jnp.mean(o * o, axis=1, keepdims=True)
        o = o * lax.rsqrt(ms + LN_EPS) * g_ref[...] * (1.0 - lam_init)
        o_ref[...] = o.astype(o_ref.dtype)

    @pl.when(qi == 0)
    def _():
        finish(chunk(None, 0, n_ctx))

    @pl.when(qi > 0)
    def _():
        carry = chunk(None, 0, n_ctx)
        for j in range(n_lat):
            carry = chunk(carry, n_ctx + j * tk, tk)
        finish(carry)


def _rope_tables(ctx_len, seq):
    n = np.arange(seq)
    pos = np.stack([n // GRID_W, n % GRID_W], axis=0).astype(np.float32)
    inv = jnp.asarray(ROPE_BASE, F32) ** (-jnp.arange(AX_DIM // 2, dtype=F32) * 2.0 / AX_DIM)
    lane = np.arange(LANES)
    ax = (lane % DA_QK_DIM) // AX_DIM
    half = (lane % AX_DIM) // (AX_DIM // 2)
    j = lane % (AX_DIM // 2)
    ang = jnp.asarray(pos)[ax, :].T * inv[j][None, :]
    cos, sin = jnp.cos(ang), jnp.sin(ang)
    sa = jnp.where(half[None, :] == 1, sin, 0.0)
    sb = jnp.where(half[None, :] == 0, -sin, 0.0)
    pad = lambda t, v: jnp.concatenate([jnp.full((ctx_len, LANES), v, F32), t.astype(F32)], axis=0)
    return pad(cos, 1.0), pad(sa, 0.0), pad(sb, 0.0)


def _diff_attention(p3, lamv, norm_g, tables, lam_init):
    B, S, _ = p3.shape
    tq = ROW_BLOCK
    n_ctx = ROW_BLOCK
    tk = _pick(S - n_ctx, (1024, 512, 256))
    cq, ck, cv = COL_Q // LANES, COL_K // LANES, COL_V // LANES
    tab_spec = pl.BlockSpec((S, LANES), lambda b, h, i: (0, 0))
    return pl.pallas_call(
        functools.partial(_attn_kernel, lam_init=lam_init, n_ctx=n_ctx, tk=tk, n_lat=(S - n_ctx) // tk, tq=tq),
        grid=(B, DA_HEADS, S // tq),
        in_specs=[pl.BlockSpec((8, LANES), lambda b, h, i: (0, 0)),
                  pl.BlockSpec((1, LANES), lambda b, h, i: (0, 0)),
                  pl.BlockSpec((None, tq, LANES), lambda b, h, i: (b, i, cq + h)),
                  pl.BlockSpec((None, S, LANES), lambda b, h, i: (b, 0, ck + h)),
                  pl.BlockSpec((None, S, LANES), lambda b, h, i: (b, 0, cv + h)),
                  tab_spec, tab_spec, tab_spec],
        out_specs=pl.BlockSpec((None, tq, LANES), lambda b, h, i: (b, i, h)),
        out_shape=jax.ShapeDtypeStruct((B, S, DA_HEADS * DA_V_DIM), BF16),
        scratch_shapes=[pltpu.VMEM((S, LANES), BF16), pltpu.VMEM((S, 2 * DA_V_DIM), BF16)],
        compiler_params=_cparams(("parallel", "parallel", "arbitrary")),
        name="diff_attention",
    )(lamv, norm_g, p3, p3, p3, *tables)


def _s5_params(a_re, a_im, log_step, b_re, b_im, c_re, c_im, d, batch):
    C = S5_CHUNK
    G, P, I = S5_GROUPS, S5_STATE, S5_GROUP
    hp = lax.Precision.HIGHEST
    dt = jnp.exp(log_step.astype(F32))[:, :, None]
    ar, ai = a_re.astype(F32), a_im.astype(F32)
    mag = jnp.exp(ar * dt)
    l_re, l_im = mag * jnp.cos(ai * dt), mag * jnp.sin(ai * dt)
    nr, ni = l_re - 1.0, l_im
    den = ar * ar + ai * ai
    f_re = (nr * ar + ni * ai) / den
    f_im = (ni * ar - nr * ai) / den
    br, bi = b_re.astype(F32), b_im.astype(F32)
    bb_re = f_re[..., None] * br - f_im[..., None] * bi
    bb_im = f_re[..., None] * bi + f_im[..., None] * br
    cr, ci = c_re.astype(F32), c_im.astype(F32)
    kpow = jnp.arange(C + 1, dtype=F32)[None, :, None, None]
    pmag = jnp.exp(kpow * (ar * dt)[:, None])
    pw_re = pmag * jnp.cos(kpow * (ai * dt)[:, None])
    pw_im = pmag * jnp.sin(kpow * (ai * dt)[:, None])
    cp_re = cr[:, None] * pw_re[:, :C, :, None, :] - ci[:, None] * pw_im[:, :C, :, None, :]
    cp_im = cr[:, None] * pw_im[:, :C, :, None, :] + ci[:, None] * pw_re[:, :C, :, None, :]
    kk = (jnp.einsum('dlgop,dgpi->dlgoi', cp_re, bb_re, precision=hp)
          - jnp.einsum('dlgop,dgpi->dlgoi', cp_im, bb_im, precision=hp))
    s_idx = np.arange(C)[:, None]
    j_idx = np.arange(C)[None, :]
    lag_f = np.clip(j_idx - s_idx, 0, C - 1)
    lag_b = np.clip(s_idx - j_idx, 0, C - 1)
    mf = jnp.where((j_idx >= s_idx)[:, :, None, None, None], kk[0][lag_f], 0.0)
    mb = jnp.where((s_idx >= j_idx)[:, :, None, None, None], kk[1][lag_b], 0.0)
    m = (mf + mb).transpose(2, 0, 4, 1, 3).reshape(G, C * I, C * I)
    def wmat(d, idx):
        pr, pi = pw_re[d][idx][..., None], pw_im[d][idx][..., None]
        w_re = pr * bb_re[d][None] - pi * bb_im[d][None]
        w_im = pr * bb_im[d][None] + pi * bb_re[d][None]
        t = lambda w: w.transpose(1, 0, 3, 2).reshape(G, C * I, P)
        return jnp.concatenate([t(w_re), t(w_im)], axis=-1)
    w = jnp.concatenate([wmat(0, C - 1 - np.arange(C)), wmat(1, np.arange(C))], axis=-1)
    def vmat(d, idx):
        pr, pi = pw_re[d][idx][:, :, None, :], pw_im[d][idx][:, :, None, :]
        v_re = cr[d][None] * pr - ci[d][None] * pi
        v_im = cr[d][None] * pi + ci[d][None] * pr
        t = lambda v: v.transpose(1, 3, 0, 2).reshape(G, P, C * I)
        return jnp.concatenate([t(v_re), -t(v_im)], axis=1)
    v = jnp.concatenate([vmat(0, 1 + np.arange(C)), vmat(1, C - np.arange(C))], axis=1)
    lc_re, lc_im = pw_re[:, C], pw_im[:, C]
    def rows(t):
        return jnp.repeat(t, batch, axis=0)
    ca = jnp.stack([rows(jnp.concatenate([lc_re[k], lc_re[k]], -1)) for k in range(2)])
    cb = jnp.stack([rows(jnp.concatenate([-lc_im[k], lc_im[k]], -1)) for k in range(2)])
    dflat = jnp.tile(d.astype(F32).reshape(G, 1, I), (1, C, 1)).reshape(G, 1, C * I)
    return m.astype(BF16), w.astype(BF16), v.astype(BF16), ca, cb, dflat


S5_GBLK = 8


def _s5_sum_kernel(u_ref, w_ref, o_ref):
    for g in range(S5_GBLK):
        o_ref[g] = jnp.dot(u_ref[g], w_ref[g], preferred_element_type=F32)


def _s5_summaries(ug, w):
    G, R, K = ug.shape
    spec = lambda n: pl.BlockSpec((S5_GBLK, n, K), lambda i: (i, 0, 0))
    return pl.pallas_call(
        _s5_sum_kernel, grid=(G // S5_GBLK,),
        in_specs=[spec(R), spec(K)], out_specs=spec(R),
        out_shape=jax.ShapeDtypeStruct((G, R, K), F32),
        compiler_params=_cparams(("parallel",)), name="s5_summaries",
    )(ug, w)


def _s5_scan_kernel(sf_ref, sb_ref, ca_ref, cb_ref, xf_out, xb_out, xf_ref, xb_ref, *, nch):
    i = pl.program_id(0)

    @pl.when(i == 0)
    def _():
        xf_ref[...] = jnp.zeros_like(xf_ref)
        xb_ref[...] = jnp.zeros_like(xb_ref)

    caf, cbf, cab, cbb = ca_ref[0], cb_ref[0], ca_ref[1], cb_ref[1]
    xf = xf_ref[...]
    xb = xb_ref[...]
    for j in range(nch):
        xf_out[j] = xf.astype(BF16)
        xf = xf * caf + pltpu.roll(xf, S5_STATE, 1) * cbf + sf_ref[j]
        jb = nch - 1 - j
        xb_out[jb] = xb.astype(BF16)
        xb = xb * cab + pltpu.roll(xb, S5_STATE, 1) * cbb + sb_ref[jb]
    xf_ref[...] = xf
    xb_ref[...] = xb


def _s5_scan(st, ca, cb):
    NC, R, _ = st.shape
    nch = ROW_BLOCK // S5_CHUNK
    nblk = NC // nch
    bwd = lambda i: jnp.where(i == 0, 0, nblk - i)
    P2 = 2 * S5_STATE
    return pl.pallas_call(
        functools.partial(_s5_scan_kernel, nch=nch), grid=(nblk,),
        in_specs=[pl.BlockSpec((nch, R, P2), lambda i: (i, 0, 0)),
                  pl.BlockSpec((nch, R, P2), lambda i: (bwd(i), 0, 1)),
                  pl.BlockSpec((2, R, P2), lambda i: (0, 0, 0)),
                  pl.BlockSpec((2, R, P2), lambda i: (0, 0, 0))],
        out_specs=[pl.BlockSpec((nch, R, P2), lambda i: (i, 0, 0)),
                   pl.BlockSpec((nch, R, P2), lambda i: (bwd(i), 0, 0))],
        out_shape=[jax.ShapeDtypeStruct((NC, R, P2), BF16)] * 2,
        scratch_shapes=[pltpu.VMEM((R, P2), F32), pltpu.VMEM((R, P2), F32)],
        compiler_params=_cparams(("arbitrary",)), name="s5_scan",
    )(st, st, ca, cb)


def _s5_out_kernel(u_ref, xp_ref, m_ref, v_ref, d_ref, o_ref):
    for g in range(S5_GBLK):
        u = u_ref[g]
        y = (jnp.dot(u, m_ref[g], preferred_element_type=F32)
             + jnp.dot(xp_ref[g], v_ref[g], preferred_element_type=F32)
             + u.astype(F32) * d_ref[g])
        o_ref[g] = _gelu(y).astype(o_ref.dtype)


def _s5_output(ug, xp, m, v, dflat):
    G, R, K = ug.shape
    spec = lambda n: pl.BlockSpec((S5_GBLK, n, K), lambda i: (i, 0, 0))
    return pl.pallas_call(
        _s5_out_kernel, grid=(G // S5_GBLK,),
        in_specs=[spec(R), spec(R), spec(K), spec(K), spec(1)], out_specs=spec(R),
        out_shape=jax.ShapeDtypeStruct((G, R, K), BF16),
        compiler_params=_cparams(("parallel",)), name="s5_output",
    )(ug, xp, m, v, dflat)


def _glu_kernel(z_ref, w_ref, b_ref, o_ref):
    z = z_ref[...]
    t = jnp.dot(z, w_ref[...], preferred_element_type=F32) + b_ref[...]
    o_ref[...] = (z.astype(F32) * _sigmoid(t)).astype(o_ref.dtype)


def _glu(z, w, b):
    T, N = z.shape
    tm = _pick(T, (1024, 768, 512, 256))
    return pl.pallas_call(
        _glu_kernel, grid=(T // tm,),
        in_specs=[pl.BlockSpec((tm, N), lambda i: (i, 0)), pl.BlockSpec((N, N), lambda i: (0, 0)),
                  pl.BlockSpec((1, N), lambda i: (0, 0))],
        out_specs=pl.BlockSpec((tm, N), lambda i: (i, 0)),
        out_shape=jax.ShapeDtypeStruct((T, N), BF16),
        compiler_params=_cparams(("parallel",)), name="s5_glu",
    )(z, w, b)


def _s5_mixer(p3, prm, glu_w, glu_b):
    B, S, _ = p3.shape
    m, w, v, ca, cb, dflat = prm
    C, G, I = S5_CHUNK, S5_GROUPS, S5_GROUP
    nc = S // C
    u = p3[:, :, COL_U:COL_U + S5_WIDTH]
    ug = u.reshape(B, nc, C, G, I).transpose(3, 0, 1, 2, 4).reshape(G, B * nc, C * I)
    s = _s5_summaries(ug, w)
    st = s.reshape(G, B, nc, 4 * S5_STATE).transpose(2, 0, 1, 3).reshape(nc, G * B, 4 * S5_STATE)
    xf, xb = _s5_scan(st, ca, cb)
    xp = jnp.concatenate([xf, xb], axis=-1).reshape(nc, G, B, 4 * S5_STATE)
    xp = xp.transpose(1, 2, 0, 3).reshape(G, B * nc, 4 * S5_STATE)
    z = _s5_output(ug, xp, m, v, dflat)
    z = z.reshape(G, B, nc, C, I).transpose(1, 2, 3, 0, 4).reshape(B * S, S5_WIDTH)
    return _glu(z, glu_w.astype(BF16), glu_b.astype(F32).reshape(1, S5_WIDTH))


def _gla_kernel(qf, kf, vf, gf, qb, kb, vb, gb, gup_ref, gbias_ref, of_ref, ob_ref, sf_ref, sb_ref, *, batch):
    i = pl.program_id(0)
    C = GLA_CHUNK

    @pl.when(i == 0)
    def _():
        sf_ref[...] = jnp.zeros_like(sf_ref)
        sb_ref[...] = jnp.zeros_like(sb_ref)

    row = lax.broadcasted_iota(jnp.int32, (C, C), 0)
    col = lax.broadcasted_iota(jnp.int32, (C, C), 1)
    dn_t = (((1,), (1,)), ((), ()))

    def one(b, d, q_ref, k_ref, v_ref, g_ref, o_ref, s_ref):
        tri = (row >= col) if d == 0 else (row <= col)
        trib = jnp.where(tri, 1.0, 0.0).astype(BF16)
        logits = jnp.dot(g_ref[b], gup_ref[d], preferred_element_type=F32) + gbias_ref[d]
        la = (jnp.minimum(logits, 0.0) - jnp.log(1.0 + jnp.exp(-jnp.abs(logits)))) * (1.0 / GLA_GATE_NORM)
        hi = la.astype(BF16)
        r1 = la - hi.astype(F32)
        mid = r1.astype(BF16)
        lo = (r1 - mid.astype(F32)).astype(BF16)
        bcum = (jnp.dot(trib, hi, preferred_element_type=F32) + jnp.dot(trib, mid, preferred_element_type=F32)
                + jnp.dot(trib, lo, preferred_element_type=F32))
        total = bcum[C - 1:C] if d == 0 else bcum[0:1]
        q = q_ref[b].astype(F32) * (GLA_DK ** -0.5)
        k = k_ref[b].astype(F32)
        qt = (q * jnp.exp(bcum)).astype(BF16)
        kt = (k * jnp.exp(-bcum)).astype(BF16)
        ks = (k * jnp.exp(total - bcum)).astype(BF16)
        et = jnp.exp(total)
        v = v_ref[b]
        for h in range(GLA_HEADS):
            sl = slice(h * GLA_DK, (h + 1) * GLA_DK)
            vs = v[:, h * GLA_DV:(h + 1) * GLA_DV]
            attn = lax.dot_general(qt[:, sl], kt[:, sl], dn_t, preferred_element_type=F32)
            attn = jnp.where(tri, attn, 0.0).astype(BF16)
            st = s_ref[b * GLA_HEADS + h]
            o = (jnp.dot(attn, vs, preferred_element_type=F32)
                 + lax.dot_general(qt[:, sl], st.astype(BF16), dn_t, preferred_element_type=F32))
            o_ref[b, :, h * GLA_DV:(h + 1) * GLA_DV] = o
            upd = lax.dot_general(vs, ks[:, sl], (((0,), (0,)), ((), ())), preferred_element_type=F32)
            s_ref[b * GLA_HEADS + h] = st * et[:, sl] + upd

    def body(b, carry):
        one(b, 0, qf, kf, vf, gf, of_ref, sf_ref)
        one(b, 1, qb, kb, vb, gb, ob_ref, sb_ref)
        return carry
    lax.fori_loop(0, batch, body, 0)


def _gla(p3, pg3, gup, gbias):
    B, S, _ = p3.shape
    C = GLA_CHUNK
    nch = S // C
    nctx = ROW_BLOCK // C
    bwd = lambda i: jnp.where(i < nctx, nctx - 1 - i, nch + nctx - 1 - i)
    wq, wv = GLA_HEADS * GLA_DK, GLA_HEADS * GLA_DV
    def specs(f):
        return [pl.BlockSpec((B, C, wq), lambda i: (0, f(i), COL_GQ // wq)),
                pl.BlockSpec((B, C, wq), lambda i: (0, f(i), COL_GK // wq)),
                pl.BlockSpec((B, C, wv), lambda i: (0, f(i), COL_GV // wv)),
                pl.BlockSpec((B, C, LANES), lambda i: (0, f(i), 0))]
    fwd = lambda i: i
    return pl.pallas_call(
        functools.partial(_gla_kernel, batch=B), grid=(nch,),
        in_specs=specs(fwd) + specs(bwd) + [pl.BlockSpec((2, LANES, wq), lambda i: (0, 0, 0)),
                                            pl.BlockSpec((2, 1, wq), lambda i: (0, 0, 0))],
        out_specs=[pl.BlockSpec((B, C, wv), lambda i: (0, i, 0)),
                   pl.BlockSpec((B, C, wv), lambda i: (0, bwd(i), 0))],
        out_shape=[jax.ShapeDtypeStruct((B, S, wv), F32)] * 2,
        scratch_shapes=[pltpu.VMEM((B * GLA_HEADS, GLA_DV, GLA_DK), F32)] * 2,
        compiler_params=_cparams(("arbitrary",)), name="gla",
    )(p3, p3, p3, pg3, p3, p3, p3, pg3, gup, gbias)


def _merge_kernel(oa_ref, ob_ref, of_ref, obk_ref, r_ref, g0_ref, g1_ref, g2_ref, gn_ref, wb_ref, y_ref):
    o = of_ref[...] + obk_ref[...]
    gn = gn_ref[...]
    parts = []
    for h in range(GLA_HEADS):
        oh = o[:, h * GLA_DV:(h + 1) * GLA_DV]
        ms = jnp.mean(oh * oh, axis=1, keepdims=True)
        parts.append(oh * lax.rsqrt(ms + LN_EPS) * gn)
    r = r_ref[...].astype(F32)
    oc = (jnp.concatenate(parts, axis=1) * (r * _sigmoid(r))).astype(BF16)
    y = (_sigmoid(g0_ref[...].astype(F32)) * jnp.dot(oa_ref[...], wb_ref[0], preferred_element_type=F32)
         + _sigmoid(g1_ref[...].astype(F32)) * jnp.dot(ob_ref[...], wb_ref[1], preferred_element_type=F32)
         + _sigmoid(g2_ref[...].astype(F32)) * jnp.dot(oc, wb_ref[2], preferred_element_type=F32))
    y_ref[...] = y.astype(y_ref.dtype)


def _merge(oa, ob, of, obk, p2, gn, wb):
    T = oa.shape[0]
    D = D_MODEL
    W = 1024
    tm = ROW_BLOCK
    row = lambda n: pl.BlockSpec((tm, n), lambda i: (i, 0))
    return pl.pallas_call(
        _merge_kernel, grid=(T // tm,),
        in_specs=[row(W), row(W), row(W), row(W),
                  pl.BlockSpec((tm, W), lambda i: (i, COL_R // W)),
                  pl.BlockSpec((tm, D), lambda i: (i, 0)),
                  pl.BlockSpec((tm, D), lambda i: (i, 1)),
                  pl.BlockSpec((tm, D), lambda i: (i, 2)),
                  pl.BlockSpec((1, GLA_DV), lambda i: (0, 0)),
                  pl.BlockSpec((3, W, D), lambda i: (0, 0, 0))],
        out_specs=row(D),
        out_shape=jax.ShapeDtypeStruct((T, D), BF16),
        compiler_params=_cparams(("parallel",)), name="merge",
    )(oa, ob, of, obk, p2, p2, p2, p2, gn, wb)


def _layer_norm(z, g, b):
    mu = jnp.mean(z, axis=1, keepdims=True)
    zc = z - mu
    var = jnp.mean(zc * zc, axis=1, keepdims=True)
    return zc * lax.rsqrt(var + LN_EPS) * g + b


def _outln_kernel(y_ref, w_ref, x_ref, g1_ref, sc_ref, sh_ref, lng_ref, lnb_ref, x_out, ht_out, *, alpha):
    mix = jnp.dot(y_ref[...], w_ref[...], preferred_element_type=F32)
    xn = _layer_norm(alpha * x_ref[...] + g1_ref[...] * mix, lng_ref[...], lnb_ref[...])
    x_out[...] = xn
    ht_out[...] = (xn * (1.0 + sc_ref[...]) + sh_ref[...]).T.astype(ht_out.dtype)


def _out_ln(y, w_out, x, mod_l, lng, lnb, nblk, ctx_row, alpha):
    T, D = x.shape
    tm = ROW_BLOCK
    row = pl.BlockSpec((tm, D), lambda i: (i, 0))
    vec = pl.BlockSpec((1, D), lambda i: (0, 0))
    return pl.pallas_call(
        functools.partial(_outln_kernel, alpha=alpha), grid=(T // tm,),
        in_specs=[row, pl.BlockSpec((D, D), lambda i: (0, 0)), row,
                  _mod_spec(2, nblk, ctx_row), _mod_spec(4, nblk, ctx_row), _mod_spec(3, nblk, ctx_row), vec, vec],
        out_specs=[row, pl.BlockSpec((D, tm), lambda i: (0, i))],
        out_shape=[jax.ShapeDtypeStruct((T, D), F32), jax.ShapeDtypeStruct((D, T), BF16)],
        compiler_params=_cparams(("parallel",)), name="out_proj_ln",
    )(y, w_out, x, mod_l, mod_l, mod_l, lng, lnb)


def _ln2_kernel(x_ref, ft_ref, g2_ref, lng_ref, lnb_ref, *rest, alpha, emit_h):
    xn = _layer_norm(alpha * x_ref[...] + g2_ref[...] * ft_ref[...].T, lng_ref[...], lnb_ref[...])
    if emit_h:
        sc_ref, sh_ref, x_out, h_out = rest
        h_out[...] = (xn * (1.0 + sc_ref[...]) + sh_ref[...]).astype(h_out.dtype)
    else:
        (x_out,) = rest
    x_out[...] = xn


def _ln2(x, ffn_t, mod_l, lng, lnb, mod_next, nblk, ctx_row, alpha):
    T, D = x.shape
    tm = ROW_BLOCK
    row = pl.BlockSpec((tm, D), lambda i: (i, 0))
    vec = pl.BlockSpec((1, D), lambda i: (0, 0))
    emit_h = mod_next is not None
    in_specs = [row, pl.BlockSpec((D, tm), lambda i: (0, i)), _mod_spec(5, nblk, ctx_row), vec, vec]
    args = [x, ffn_t, mod_l, lng, lnb]
    out_specs, out_shape = [row], [jax.ShapeDtypeStruct((T, D), F32)]
    if emit_h:
        in_specs += [_mod_spec(1, nblk, ctx_row), _mod_spec(0, nblk, ctx_row)]
        args += [mod_next, mod_next]
        out_specs.append(row)
        out_shape.append(jax.ShapeDtypeStruct((T, D), BF16))
    return pl.pallas_call(
        functools.partial(_ln2_kernel, alpha=alpha, emit_h=emit_h), grid=(T // tm,),
        in_specs=in_specs, out_specs=out_specs, out_shape=out_shape,
        compiler_params=_cparams(("parallel",)), name="ffn_ln",
    )(*args)


def _peer_scores_kernel(h_ref, wq_ref, k1_ref, k2_ref, s_ref):
    qt = jnp.dot(wq_ref[...], h_ref[...], preferred_element_type=F32).astype(BF16)
    n = PEER_NKEYS
    for h in range(PEER_HEADS):
        r = 2 * n * h
        s_ref[r:r + n, :] = jnp.dot(k1_ref[h], qt[r:r + n, :], preferred_element_type=F32)
        s_ref[r + n:r + 2 * n, :] = jnp.dot(k2_ref[h], qt[r + n:r + 2 * n, :], preferred_element_type=F32)


def _peer_scores(ht, wqt, k1, k2):
    D, T = ht.shape
    tt = _pick(T, (512, 256))
    Q = wqt.shape[0]
    n = PEER_NKEYS
    return pl.pallas_call(
        _peer_scores_kernel, grid=(T // tt,),
        in_specs=[pl.BlockSpec((D, tt), lambda i: (0, i)), pl.BlockSpec((Q, D), lambda i: (0, 0)),
                  pl.BlockSpec((PEER_HEADS, n, n), lambda i: (0, 0, 0)),
                  pl.BlockSpec((PEER_HEADS, n, n), lambda i: (0, 0, 0))],
        out_specs=pl.BlockSpec((Q, tt), lambda i: (0, i)),
        out_shape=jax.ShapeDtypeStruct((Q, T), F32),
        compiler_params=_cparams(("parallel",)), name="peer_scores",
    )(ht, wqt, k1, k2)


def _peer_select_kernel(s_ref, nt_ref, e1_ref, s2_ref, e2_ref):
    n, K = PEER_NKEYS, PEER_TOPK
    neg = -jnp.inf

    def top_values(s, count):
        vals = []
        for _ in range(count):
            m = jnp.max(s, axis=0, keepdims=True)
            vals.append(m)
            s = jnp.where(s == m, neg, s)
        return vals

    def body(h, carry):
        r1 = pl.multiple_of(h * 2 * n, 2 * n)
        r2 = pl.multiple_of(h * 2 * n + n, n)
        ro = pl.multiple_of(h * n, n)
        s1 = s_ref[pl.ds(r1, n), :]
        s2 = s_ref[pl.ds(r2, n), :]
        v1 = top_values(s1, K + 1)
        v2 = top_values(s2, K + 1)
        half = K // 2
        v1m = jnp.concatenate(v1[half:K], axis=0)
        v2m = jnp.concatenate(v2[:K], axis=0)
        cand = jnp.concatenate([v1[0] + v2m] + [v1[a] + v2m[:half] for a in range(1, half)]
                               + [v1m + v2[0]], axis=0)
        tops = top_values(cand, K + 1)
        nxt = jnp.maximum(tops[K], jnp.maximum(v1[K] + v2[0], v1[0] + v2[K]))
        tau = 0.5 * (tops[K - 1] + nxt)
        mtop = v1[0] + v2[0]
        z = jnp.sum(jnp.where(cand >= tau, jnp.exp(cand - mtop), 0.0), axis=0, keepdims=True)
        nt_ref[pl.ds(ro, n), :] = tau - s1
        e1_ref[pl.ds(ro, n), :] = jnp.exp(s1 - v1[0]) / z
        s2_ref[pl.ds(ro, n), :] = s2
        e2_ref[pl.ds(ro, n), :] = jnp.exp(s2 - v2[0])
        return carry
    lax.fori_loop(0, PEER_HEADS, body, 0)


def _peer_select(st):
    Q, T = st.shape
    tt = _pick(T, (256, 128))
    R = PEER_HEADS * PEER_NKEYS
    out = pl.BlockSpec((R, tt), lambda i: (0, i))
    return pl.pallas_call(
        _peer_select_kernel, grid=(T // tt,),
        in_specs=[pl.BlockSpec((Q, tt), lambda i: (0, i))],
        out_specs=[out] * 4,
        out_shape=[jax.ShapeDtypeStruct((R, T), F32)] * 4,
        compiler_params=_cparams(("parallel",)), name="peer_select",
    )(st)


def _peer_main_kernel(h_ref, u_ref, vt_ref, nt_ref, e1_ref, s2_ref, e2_ref, o_ref, act_a, act_b, p_a, p_b,
                      *, n_i1, n_e):
    e = pl.program_id(1)
    n = PEER_NKEYS

    @pl.when(e == 0)
    def _():
        o_ref[...] = jnp.zeros_like(o_ref)
        act_b[...] = jnp.zeros_like(act_b)
        p_a[...] = jnp.zeros_like(p_a)

    def step(act_w, act_r, p_w, p_r):
        o_ref[...] += jnp.dot(vt_ref[...], p_r[...], preferred_element_type=F32)

        tile = jnp.clip(e - 1, 0, n_e - 1)
        nr = 4
        rh = n // nr
        for c in range(h_ref.shape[1] // LANES):
            cs = slice(c * LANES, (c + 1) * LANES)
            for r in range(nr):
                acc = [None] * n_i1
                for h in range(PEER_HEADS):
                    rs = slice(h * n + r * rh, h * n + (r + 1) * rh)
                    s2 = s2_ref[rs, cs]
                    e2 = e2_ref[rs, cs]
                    for a in range(n_i1):
                        row = h * n + tile * n_i1 + a
                        ntr = nt_ref[pl.ds(row, 1), :][:, cs]
                        e1r = e1_ref[pl.ds(row, 1), :][:, cs]
                        g = jnp.where(s2 >= ntr, e1r * e2, 0.0)
                        acc[a] = g if acc[a] is None else acc[a] + g
                for a in range(n_i1):
                    ps = slice(a * n + r * rh, a * n + (r + 1) * rh)
                    p_w[ps, cs] = (_gelu(act_r[ps, cs]) * acc[a]).astype(p_w.dtype)

        act_w[...] = jnp.dot(u_ref[...], h_ref[...], preferred_element_type=F32)

    @pl.when(e % 2 == 0)
    def _():
        step(act_a, act_b, p_b, p_a)

    @pl.when(e % 2 == 1)
    def _():
        step(act_b, act_a, p_a, p_b)


def _peer_main(ht, ub, vtb, nt, e1, s2, e2):
    D, T = ht.shape
    E = ub.shape[0]
    tb = _pick(T, (512, 256))
    te = 512
    n_e = E // te
    R = PEER_HEADS * PEER_NKEYS
    sel = pl.BlockSpec((R, tb), lambda i, e: (0, i))
    return pl.pallas_call(
        functools.partial(_peer_main_kernel, n_i1=te // PEER_NKEYS, n_e=n_e), grid=(T // tb, n_e + 2),
        in_specs=[pl.BlockSpec((D, tb), lambda i, e: (0, i)),
                  pl.BlockSpec((te, D), lambda i, e: (jnp.minimum(e, n_e - 1), 0)),
                  pl.BlockSpec((D, te), lambda i, e: (0, jnp.clip(e - 2, 0, n_e - 1))),
                  sel, sel, sel, sel],
        out_specs=pl.BlockSpec((D, tb), lambda i, e: (0, i)),
        out_shape=jax.ShapeDtypeStruct((D, T), F32),
        scratch_shapes=[pltpu.VMEM((te, tb), F32), pltpu.VMEM((te, tb), F32),
                        pltpu.VMEM((te, tb), BF16), pltpu.VMEM((te, tb), BF16)],
        compiler_params=_cparams(("parallel", "arbitrary")), name="peer_main",
    )(ht, ub, vtb, nt, e1, s2, e2)


def _peer(ht, wq, k1, k2, u_tab, v_tab):
    st = _peer_scores(ht, wq.T.astype(BF16), k1.astype(BF16), k2.astype(BF16))
    nt, e1, s2, e2 = _peer_select(st)
    return _peer_main(ht, u_tab.astype(BF16), v_tab.T.astype(BF16), nt, e1, s2, e2)


def kernel(x, c, ctx, c_ctx, w_ada, b_ada, w_in, lam_q1, lam_k1, lam_q2, lam_k2, da_norm_g, s5_a_re, s5_a_im, s5_log_step, s5_b_re, s5_b_im, s5_c_re, s5_c_im, s5_d, s5_glu_w, s5_glu_b, gla_gate_up, gla_gate_b, gla_norm_g, w_branch, w_out, ln1_g, ln1_b, ln2_g, ln2_b, peer_wq, peer_k1, peer_k2, peer_u, peer_v):
    B, SEQ, D = x.shape
    CTX = ctx.shape[1]
    L = w_ada.shape[0]
    assert CTX == ROW_BLOCK and SEQ % ROW_BLOCK == 0 and D == D_MODEL and B < 8
    S = CTX + SEQ
    T = B * S
    nblk = S // ROW_BLOCK
    ctx_row = B
    alpha = (2 * L) ** 0.25

    c8 = jnp.zeros((8, D), F32).at[:B].set(c.astype(F32)).at[B].set(c_ctx.astype(F32))
    mod = _ada_mod(c8, w_ada, b_ada).reshape(L, 8, 6, 1, D)
    tables = _rope_tables(CTX, SEQ)
    xs = jnp.concatenate([ctx, x], axis=1).reshape(T, D).astype(F32)

    sizes = np.cumsum([0, 1024, 1024, 1024, 1024, 512, 512, 1024, 16, 16, 1024, 6144])
    seg = lambda w, k: w[:, sizes[k]:sizes[k + 1]]

    for l in range(L):
        lam_init = 0.8 - 0.6 * math.exp(-0.3 * l)
        mod_l = mod[l]
        wl = w_in[l]
        w_main = jnp.concatenate([seg(wl, 10), seg(wl, 0), seg(wl, 1), seg(wl, 2), seg(wl, 3), seg(wl, 4),
                                  seg(wl, 5), seg(wl, 6), seg(wl, 9)], axis=1).astype(BF16)
        w_gate = jnp.concatenate([seg(wl, 7), seg(wl, 8), jnp.zeros((D, LANES - 2 * GLA_RANK), wl.dtype)],
                                 axis=1).astype(BF16)
        if l == 0:
            h = _modulate(xs, mod_l, nblk, ctx_row)
        p2 = _matmul(h, w_main, BF16, "in_proj")
        pg = _matmul(h, w_gate, BF16, "in_proj_gates")
        p3 = p2.reshape(B, S, N_MAIN)

        lamv = jnp.zeros((8, LANES), F32)
        for r, t in enumerate((lam_q1[l], lam_k1[l], lam_q2[l], lam_k2[l])):
            lamv = lamv.at[r, :DA_QK_DIM].set(t.astype(F32))
        oa = _diff_attention(p3, lamv, da_norm_g[l].astype(F32).reshape(1, DA_V_DIM), tables, lam_init)

        prm = _s5_params(s5_a_re[l], s5_a_im[l], s5_log_step[l], s5_b_re[l], s5_b_im[l],
                         s5_c_re[l], s5_c_im[l], s5_d[l], B)
        ob = _s5_mixer(p3, prm, s5_glu_w[l], s5_glu_b[l])

        gup = jnp.zeros((2, LANES, GLA_HEADS * GLA_DK), F32)
        gup = gup.at[0, :GLA_RANK].set(gla_gate_up[l, 0]).at[1, GLA_RANK:2 * GLA_RANK].set(gla_gate_up[l, 1])
        gbias = gla_gate_b[l].astype(F32).reshape(2, 1, GLA_HEADS * GLA_DK)
        of, obk = _gla(p3, pg.reshape(B, S, LANES), gup.astype(BF16), gbias)

        y = _merge(oa.reshape(T, -1), ob, of.reshape(T, -1), obk.reshape(T, -1), p2,
                   gla_norm_g[l].astype(F32).reshape(1, GLA_DV), w_branch[l].astype(BF16))
        x1, h2t = _out_ln(y, w_out[l].astype(BF16), xs, mod_l, ln1_g[l].reshape(1, D), ln1_b[l].reshape(1, D),
                          nblk, ctx_row, alpha)
        ffn_t = _peer(h2t, peer_wq[l], peer_k1[l], peer_k2[l], peer_u[l], peer_v[l])
        res = _ln2(x1, ffn_t, mod_l, ln2_g[l].reshape(1, D), ln2_b[l].reshape(1, D),
                   mod[l + 1] if l + 1 < L else None, nblk, ctx_row, alpha)
        xs = res[0]
        if l + 1 < L:
            h = res[1]

    return xs.reshape(B, S, D)[:, CTX:, :]
```

```python
import functools
import math

import jax
import jax.numpy as jnp
import numpy as np
from jax import lax
from jax.experimental import pallas as pl
from jax.experimental.pallas import tpu as pltpu

F32 = jnp.float32
BF16 = jnp.bfloat16

D_MODEL = 2048
GRID_W = 64
DA_HEADS = 8
DA_QK_DIM = 64
DA_V_DIM = 128
AX_DIM = DA_QK_DIM // 2
ROPE_BASE = 10000.0
S5_WIDTH = 1024
S5_GROUP = 16
S5_GROUPS = 64
S5_STATE = 64
GLA_HEADS = 4
GLA_DK = 128
GLA_DV = 256
GLA_RANK = 16
GLA_CHUNK = 64
GLA_GATE_NORM = 16.0
PEER_HEADS = 8
PEER_NKEYS = 128
PEER_TOPK = 16
LN_EPS = 1e-5

LANES = 128
ROW_BLOCK = 256
S5_CHUNK = 16
VMEM_LIMIT = 56 * 1024 * 1024

COL_MG = 0
COL_Q = 6144
COL_K = 7168
COL_V = 8192
COL_U = 9216
COL_GQ = 10240
COL_GK = 10752
COL_GV = 11264
COL_R = 12288
N_MAIN = 13312


def _cparams(sem):
    return pltpu.CompilerParams(dimension_semantics=sem, vmem_limit_bytes=VMEM_LIMIT)


def _pick(n, cands):
    for c in cands:
        if n % c == 0:
            return c
    raise ValueError(f"no tile for {n}")


def _gelu(x):
    return 0.5 * x * (1.0 + jnp.tanh(math.sqrt(2.0 / math.pi) * (x + 0.044715 * (x * x * x))))


def _sigmoid(x):
    return 1.0 / (1.0 + jnp.exp(-x))


def _ada_kernel(c_ref, w_ref, b_ref, o_ref):
    c = c_ref[...]
    s = (c * _sigmoid(c)).astype(BF16)
    o_ref[...] = jnp.dot(s, w_ref[...].astype(BF16), preferred_element_type=F32) + b_ref[...]


def _ada_mod(c8, w_ada, b_ada):
    L, D, N = w_ada.shape
    tn = 1536
    return pl.pallas_call(
        _ada_kernel,
        grid=(L, N // tn),
        in_specs=[pl.BlockSpec((8, D), lambda l, j: (0, 0)),
                  pl.BlockSpec((None, D, tn), lambda l, j: (l, 0, j)),
                  pl.BlockSpec((None, 1, tn), lambda l, j: (l, 0, j))],
        out_specs=pl.BlockSpec((None, 8, tn), lambda l, j: (l, 0, j)),
        out_shape=jax.ShapeDtypeStruct((L, 8, N), F32),
        compiler_params=_cparams(("parallel", "parallel")),
        name="ada_mod",
    )(c8, w_ada, b_ada.reshape(L, 1, N))


def _mod_spec(k, nblk, ctx_row):
    return pl.BlockSpec((None, None, 1, D_MODEL),
                        lambda i: (jnp.where(i % nblk == 0, ctx_row, i // nblk), k, 0, 0))


def _modulate_kernel(x_ref, sc_ref, sh_ref, o_ref):
    o_ref[...] = (x_ref[...] * (1.0 + sc_ref[...]) + sh_ref[...]).astype(o_ref.dtype)


def _modulate(x, mod_l, nblk, ctx_row):
    T, D = x.shape
    return pl.pallas_call(
        _modulate_kernel,
        grid=(T // ROW_BLOCK,),
        in_specs=[pl.BlockSpec((ROW_BLOCK, D), lambda i: (i, 0)),
                  _mod_spec(1, nblk, ctx_row), _mod_spec(0, nblk, ctx_row)],
        out_specs=pl.BlockSpec((ROW_BLOCK, D), lambda i: (i, 0)),
        out_shape=jax.ShapeDtypeStruct((T, D), BF16),
        compiler_params=_cparams(("parallel",)),
        name="modulate",
    )(x, mod_l, mod_l)


def _mm_kernel(a_ref, b_ref, o_ref):
    o_ref[...] = jnp.dot(a_ref[...], b_ref[...], preferred_element_type=F32).astype(o_ref.dtype)


def _matmul(a, b, out_dtype, name):
    M, K = a.shape
    N = b.shape[1]
    tm = _pick(M, (1024, 768, 512, 256))
    tn = _pick(N, (1024, 512, 256, 128))
    return pl.pallas_call(
        _mm_kernel,
        grid=(M // tm, N // tn),
        in_specs=[pl.BlockSpec((tm, K), lambda i, j: (i, 0)),
                  pl.BlockSpec((K, tn), lambda i, j: (0, j))],
        out_specs=pl.BlockSpec((tm, tn), lambda i, j: (i, j)),
        out_shape=jax.ShapeDtypeStruct((M, N), out_dtype),
        compiler_params=_cparams(("parallel", "parallel")),
        name=name,
    )(a, b)


def _attn_kernel(lam_ref, g_ref, q_ref, k_ref, v_ref, cos_ref, sa_ref, sb_ref, o_ref, kr_ref, ve_ref,
                 *, lam_init, n_ctx, tk, n_lat, tq):
    qi = pl.program_id(2)
    n_rows = n_ctx + n_lat * tk

    def rope(x, r0, n):
        c = cos_ref[pl.ds(r0, n), :]
        sa = sa_ref[pl.ds(r0, n), :]
        sb = sb_ref[pl.ds(r0, n), :]
        return x * c + pltpu.roll(x, 16, 1) * sa + pltpu.roll(x, LANES - 16, 1) * sb

    @pl.when(qi == 0)
    def _():
        def body(j, carry):
            r0 = pl.multiple_of(j * tq, tq)
            kr_ref[pl.ds(r0, tq), :] = rope(k_ref[pl.ds(r0, tq), :].astype(F32), r0, tq).astype(BF16)
            ve_ref[pl.ds(r0, tq), 0:DA_V_DIM] = v_ref[pl.ds(r0, tq), :]
            ve_ref[pl.ds(r0, tq), DA_V_DIM:2 * DA_V_DIM] = jnp.ones((tq, DA_V_DIM), BF16)
            return carry
        lax.fori_loop(0, n_rows // tq, body, 0)

    lv = lam_ref[...]
    lam = (jnp.exp(jnp.sum(lv[0:1] * lv[1:2], axis=1, keepdims=True))
           - jnp.exp(jnp.sum(lv[2:3] * lv[3:4], axis=1, keepdims=True)) + lam_init)

    q = rope(q_ref[...].astype(F32), pl.multiple_of(qi * tq, tq), tq) * (DA_QK_DIM ** -0.5 * math.log2(math.e))
    lane = lax.broadcasted_iota(jnp.int32, q.shape, 1)
    qs = (jnp.where(lane < DA_QK_DIM, q, 0.0).astype(BF16), jnp.where(lane >= DA_QK_DIM, q, 0.0).astype(BF16))

    def chunk(carry, r0, n):
        kj = kr_ref[r0:r0 + n, :]
        vj = ve_ref[r0:r0 + n, :]
        out = []
        for c in range(2):
            s = lax.dot_general(qs[c], kj, (((1,), (1,)), ((), ())), preferred_element_type=F32)
            mx = jnp.max(s, axis=1, keepdims=True)
            if carry is None:
                mn = mx
                acc = jnp.dot(jnp.exp2(s - mn).astype(BF16), vj, preferred_element_type=F32)
            else:
                m, acc = carry[c]
                mn = jnp.maximum(m, mx)
                acc = jnp.exp2(m - mn) * acc + jnp.dot(jnp.exp2(s - mn).astype(BF16), vj,
                                                      preferred_element_type=F32)
            out.append((mn, acc))
        return out

    def finish(carry):
        (_, acc_a), (_, acc_b) = carry
        o = (acc_a[:, :DA_V_DIM] / acc_a[:, DA_V_DIM:] - lam * (acc_b[:, :DA_V_DIM] / acc_b[:, DA_V_DIM:]))
        ms = jnp.mean(o * o, axis=1, keepdims=True)
        o = o * lax.rsqrt(ms + LN_EPS) * g_ref[...] * (1.0 - lam_init)
        o_ref[...] = o.astype(o_ref.dtype)

    @pl.when(qi == 0)
    def _():
        finish(chunk(None, 0, n_ctx))

    @pl.when(qi > 0)
    def _():
        carry = chunk(None, 0, n_ctx)
        for j in range(n_lat):
            carry = chunk(carry, n_ctx + j * tk, tk)
        finish(carry)


def _rope_tables(ctx_len, seq):
    n = np.arange(seq)
    pos = np.stack([n // GRID_W, n % GRID_W], axis=0).astype(np.float32)
    inv = jnp.asarray(ROPE_BASE, F32) ** (-jnp.arange(AX_DIM // 2, dtype=F32) * 2.0 / AX_DIM)
    lane = np.arange(LANES)
    ax = (lane % DA_QK_DIM) // AX_DIM
    half = (lane % AX_DIM) // (AX_DIM // 2)
    j = lane % (AX_DIM // 2)
    ang = jnp.asarray(pos)[ax, :].T * inv[j][None, :]
    cos, sin = jnp.cos(ang), jnp.sin(ang)
    sa = jnp.where(half[None, :] == 1, sin, 0.0)
    sb = jnp.where(half[None, :] == 0, -sin, 0.0)
    pad = lambda t, v: jnp.concatenate([jnp.full((ctx_len, LANES), v, F32), t.astype(F32)], axis=0)
    return pad(cos, 1.0), pad(sa, 0.0), pad(sb, 0.0)


def _diff_attention(p3, lamv, norm_g, tables, lam_init):
    B, S, _ = p3.shape
    tq = ROW_BLOCK
    n_ctx = ROW_BLOCK
    tk = _pick(S - n_ctx, (1024, 512, 256))
    cq, ck, cv = COL_Q // LANES, COL_K // LANES, COL_V // LANES
    tab_spec = pl.BlockSpec((S, LANES), lambda b, h, i: (0, 0))
    return pl.pallas_call(
        functools.partial(_attn_kernel, lam_init=lam_init, n_ctx=n_ctx, tk=tk, n_lat=(S - n_ctx) // tk, tq=tq),
        grid=(B, DA_HEADS, S // tq),
        in_specs=[pl.BlockSpec((8, LANES), lambda b, h, i: (0, 0)),
                  pl.BlockSpec((1, LANES), lambda b, h, i: (0, 0)),
                  pl.BlockSpec((None, tq, LANES), lambda b, h, i: (b, i, cq + h)),
                  pl.BlockSpec((None, S, LANES), lambda b, h, i: (b, 0, ck + h)),
                  pl.BlockSpec((None, S, LANES), lambda b, h, i: (b, 0, cv + h)),
                  tab_spec, tab_spec, tab_spec],
        out_specs=pl.BlockSpec((None, tq, LANES), lambda b, h, i: (b, i, h)),
        out_shape=jax.ShapeDtypeStruct((B, S, DA_HEADS * DA_V_DIM), BF16),
        scratch_shapes=[pltpu.VMEM((S, LANES), BF16), pltpu.VMEM((S, 2 * DA_V_DIM), BF16)],
        compiler_params=_cparams(("parallel", "parallel", "arbitrary")),
        name="diff_attention",
    )(lamv, norm_g, p3, p3, p3, *tables)


def _s5_params(a_re, a_im, log_step, b_re, b_im, c_re, c_im, d, batch):
    C = S5_CHUNK
    G, P, I = S5_GROUPS, S5_STATE, S5_GROUP
    hp = lax.Precision.HIGHEST
    dt = jnp.exp(log_step.astype(F32))[:, :, None]
    ar, ai = a_re.astype(F32), a_im.astype(F32)
    mag = jnp.exp(ar * dt)
    l_re, l_im = mag * jnp.cos(ai * dt), mag * jnp.sin(ai * dt)
    nr, ni = l_re - 1.0, l_im
    den = ar * ar + ai * ai
    f_re = (nr * ar + ni * ai) / den
    f_im = (ni * ar - nr * ai) / den
    br, bi = b_re.astype(F32), b_im.astype(F32)
    bb_re = f_re[..., None] * br - f_im[..., None] * bi
    bb_im = f_re[..., None] * bi + f_im[..., None] * br
    cr, ci = c_re.astype(F32), c_im.astype(F32)
    kpow = jnp.arange(C + 1, dtype=F32)[None, :, None, None]
    pmag = jnp.exp(kpow * (ar * dt)[:, None])
    pw_re = pmag * jnp.cos(kpow * (ai * dt)[:, None])
    pw_im = pmag * jnp.sin(kpow * (ai * dt)[:, None])
    cp_re = cr[:, None] * pw_re[:, :C, :, None, :] - ci[:, None] * pw_im[:, :C, :, None, :]
    cp_im = cr[:, None] * pw_im[:, :C, :, None, :] + ci[:, None] * pw_re[:, :C, :, None, :]
    kk = (jnp.einsum('dlgop,dgpi->dlgoi', cp_re, bb_re, precision=hp)
          - jnp.einsum('dlgop,dgpi->dlgoi', cp_im, bb_im, precision=hp))
    lag = np.arange(C)[:, None, None]
    s_idx = np.arange(C)[None, :, None]
    j_idx = np.arange(C)[None, None, :]
    sel_f = jnp.asarray((j_idx - s_idx == lag).astype(np.float32))
    sel_b = jnp.asarray((s_idx - j_idx == lag).astype(np.float32))
    mf = jnp.einsum('lsj,lgoi->sjgoi', sel_f, kk[0], precision=hp)
    mb = jnp.einsum('lsj,lgoi->sjgoi', sel_b, kk[1], precision=hp)
    m = (mf + mb).transpose(2, 0, 4, 1, 3).reshape(G, C * I, C * I)
    def wmat(d, pick):
        pr, pi = pick(pw_re[d])[..., None], pick(pw_im[d])[..., None]
        w_re = pr * bb_re[d][None] - pi * bb_im[d][None]
        w_im = pr * bb_im[d][None] + pi * bb_re[d][None]
        t = lambda w: w.transpose(1, 0, 3, 2).reshape(G, C * I, P)
        return jnp.concatenate([t(w_re), t(w_im)], axis=-1)
    w = jnp.concatenate([wmat(0, lambda t: t[:C][::-1]), wmat(1, lambda t: t[:C])], axis=-1)
    def vmat(d, pick):
        pr, pi = pick(pw_re[d])[:, :, None, :], pick(pw_im[d])[:, :, None, :]
        v_re = cr[d][None] * pr - ci[d][None] * pi
        v_im = cr[d][None] * pi + ci[d][None] * pr
        t = lambda v: v.transpose(1, 3, 0, 2).reshape(G, P, C * I)
        return jnp.concatenate([t(v_re), -t(v_im)], axis=1)
    v = jnp.concatenate([vmat(0, lambda t: t[1:]), vmat(1, lambda t: t[1:][::-1])], axis=1)
    lc_re, lc_im = pw_re[:, C], pw_im[:, C]
    def rows(t):
        return jnp.repeat(t, batch, axis=0)
    ca = jnp.stack([rows(jnp.concatenate([lc_re[k], lc_re[k]], -1)) for k in range(2)])
    cb = jnp.stack([rows(jnp.concatenate([-lc_im[k], lc_im[k]], -1)) for k in range(2)])
    dflat = jnp.tile(d.astype(F32).reshape(G, 1, I), (1, C, 1)).reshape(G, 1, C * I)
    return m.astype(BF16), w.astype(BF16), v.astype(BF16), ca, cb, dflat


S5_GBLK = LANES // S5_GROUP


def _s5_lane_perm():
    C, I = S5_CHUNK, S5_GROUP
    r = np.arange(C * LANES)
    s_, g_, i_ = r // LANES, (r % LANES) // I, r % I
    dst = g_ * (C * I) + s_ * I + i_
    return (jnp.asarray(dst)[:, None] == jnp.arange(C * LANES)[None, :]).astype(BF16)


def _s5_sum_kernel(x_ref, perm_ref, w_ref, s_ref, ug_ref):
    K = S5_CHUNK * S5_GROUP
    up = jnp.dot(x_ref[...], perm_ref[...], preferred_element_type=F32).astype(ug_ref.dtype)
    for g in range(S5_GBLK):
        ug = up[:, g * K:(g + 1) * K]
        ug_ref[g] = ug
        s_ref[g] = jnp.dot(ug, w_ref[g], preferred_element_type=F32)


def _s5_summaries(x, perm, w):
    NB, R, KX = x.shape
    K = w.shape[1]
    tr = R // 2
    grp = lambda n: pl.BlockSpec((S5_GBLK, n, K), lambda i, j: (i, 0, 0))
    out = pl.BlockSpec((S5_GBLK, tr, K), lambda i, j: (i, j, 0))
    return pl.pallas_call(
        _s5_sum_kernel, grid=(NB, 2),
        in_specs=[pl.BlockSpec((None, tr, KX), lambda i, j: (i, j, 0)),
                  pl.BlockSpec((KX, KX), lambda i, j: (0, 0)), grp(K)],
        out_specs=[out, out],
        out_shape=[jax.ShapeDtypeStruct((NB * S5_GBLK, R, K), F32), jax.ShapeDtypeStruct((NB * S5_GBLK, R, K), BF16)],
        compiler_params=_cparams(("parallel", "parallel")), name="s5_summaries",
    )(x, perm, w)


def _s5_scan_kernel(sf_ref, sb_ref, ca_ref, cb_ref, xf_out, xb_out, xf_ref, xb_ref, *, nch):
    i = pl.program_id(0)

    @pl.when(i == 0)
    def _():
        xf_ref[...] = jnp.zeros_like(xf_ref)
        xb_ref[...] = jnp.zeros_like(xb_ref)

    caf, cbf, cab, cbb = ca_ref[0], cb_ref[0], ca_ref[1], cb_ref[1]
    xf = xf_ref[...]
    xb = xb_ref[...]
    for j in range(nch):
        xf_out[j] = xf.astype(BF16)
        xf = xf * caf + pltpu.roll(xf, S5_STATE, 1) * cbf + sf_ref[j]
        jb = nch - 1 - j
        xb_out[jb] = xb.astype(BF16)
        xb = xb * cab + pltpu.roll(xb, S5_STATE, 1) * cbb + sb_ref[jb]
    xf_ref[...] = xf
    xb_ref[...] = xb


def _s5_scan(st, ca, cb):
    NC, R, _ = st.shape
    nch = ROW_BLOCK // S5_CHUNK
    nblk = NC // nch
    bwd = lambda i: jnp.where(i == 0, 0, nblk - i)
    P2 = 2 * S5_STATE
    return pl.pallas_call(
        functools.partial(_s5_scan_kernel, nch=nch), grid=(nblk,),
        in_specs=[pl.BlockSpec((nch, R, P2), lambda i: (i, 0, 0)),
                  pl.BlockSpec((nch, R, P2), lambda i: (bwd(i), 0, 1)),
                  pl.BlockSpec((2, R, P2), lambda i: (0, 0, 0)),
                  pl.BlockSpec((2, R, P2), lambda i: (0, 0, 0))],
        out_specs=[pl.BlockSpec((nch, R, P2), lambda i: (i, 0, 0)),
                   pl.BlockSpec((nch, R, P2), lambda i: (bwd(i), 0, 0))],
        out_shape=[jax.ShapeDtypeStruct((NC, R, P2), BF16)] * 2,
        scratch_shapes=[pltpu.VMEM((R, P2), F32), pltpu.VMEM((R, P2), F32)],
        compiler_params=_cparams(("arbitrary",)), name="s5_scan",
    )(st, st, ca, cb)


def _s5_out_kernel(u_ref, xp_ref, m_ref, v_ref, d_ref, perm_ref, o_ref):
    zs = []
    for g in range(S5_GBLK):
        u = u_ref[g]
        y = (jnp.dot(u, m_ref[g], preferred_element_type=F32)
             + jnp.dot(xp_ref[g], v_ref[g], preferred_element_type=F32)
             + u.astype(F32) * d_ref[g])
        zs.append(_gelu(y).astype(o_ref.dtype))
    z = jnp.concatenate(zs, axis=1)
    o_ref[...] = jnp.dot(z, perm_ref[...], preferred_element_type=F32).astype(o_ref.dtype)


def _s5_output(ug, xp, m, v, dflat, perm_t):
    G, R, K = ug.shape
    KX = perm_t.shape[0]
    tr = R // 2
    rows = lambda: pl.BlockSpec((S5_GBLK, tr, K), lambda i, j: (i, j, 0))
    grp = lambda n: pl.BlockSpec((S5_GBLK, n, K), lambda i, j: (i, 0, 0))
    return pl.pallas_call(
        _s5_out_kernel, grid=(G // S5_GBLK, 2),
        in_specs=[rows(), rows(), grp(K), grp(K), grp(1), pl.BlockSpec((KX, KX), lambda i, j: (0, 0))],
        out_specs=pl.BlockSpec((None, tr, KX), lambda i, j: (i, j, 0)),
        out_shape=jax.ShapeDtypeStruct((G // S5_GBLK, R, KX), BF16),
        compiler_params=_cparams(("parallel", "parallel")), name="s5_output",
    )(ug, xp, m, v, dflat, perm_t)


def _glu_kernel(z_ref, w_ref, b_ref, o_ref):
    z = z_ref[...]
    t = jnp.dot(z, w_ref[...], preferred_element_type=F32) + b_ref[...]
    o_ref[...] = (z.astype(F32) * _sigmoid(t)).astype(o_ref.dtype)


def _glu(z, w, b):
    T, N = z.shape
    tm = _pick(T, (1024, 768, 512, 256))
    return pl.pallas_call(
        _glu_kernel, grid=(T // tm,),
        in_specs=[pl.BlockSpec((tm, N), lambda i: (i, 0)), pl.BlockSpec((N, N), lambda i: (0, 0)),
                  pl.BlockSpec((1, N), lambda i: (0, 0))],
        out_specs=pl.BlockSpec((tm, N), lambda i: (i, 0)),
        out_shape=jax.ShapeDtypeStruct((T, N), BF16),
        compiler_params=_cparams(("parallel",)), name="s5_glu",
    )(z, w, b)


def _s5_mixer(p3, prm, glu_w, glu_b):
    B, S, _ = p3.shape
    m, w, v, ca, cb, dflat = prm
    C, G, I = S5_CHUNK, S5_GROUPS, S5_GROUP
    nc = S // C
    u = p3[:, :, COL_U:COL_U + S5_WIDTH]
    nb = S5_WIDTH // LANES
    x = u.reshape(B * nc, C, nb, LANES).transpose(2, 0, 1, 3).reshape(nb, B * nc, C * LANES)
    perm = _s5_lane_perm()
    s, ug = _s5_summaries(x, perm, w)
    st = s.reshape(G, B, nc, 4 * S5_STATE).transpose(2, 0, 1, 3).reshape(nc, G * B, 4 * S5_STATE)
    xf, xb = _s5_scan(st, ca, cb)
    xp = jnp.concatenate([xf, xb], axis=-1).reshape(nc, G, B, 4 * S5_STATE)
    xp = xp.transpose(1, 2, 0, 3).reshape(G, B * nc, 4 * S5_STATE)
    z = _s5_output(ug, xp, m, v, dflat, perm.T)
    z = z.reshape(nb, B * nc, C, LANES).transpose(1, 2, 0, 3).reshape(B * S, S5_WIDTH)
    return _glu(z, glu_w.astype(BF16), glu_b.astype(F32).reshape(1, S5_WIDTH))


def _gla_kernel(qf, kf, vf, gf, qb, kb, vb, gb, gup_ref, gbias_ref, of_ref, ob_ref, sf_ref, sb_ref, *, batch):
    i = pl.program_id(0)
    C = GLA_CHUNK

    @pl.when(i == 0)
    def _():
        sf_ref[...] = jnp.zeros_like(sf_ref)
        sb_ref[...] = jnp.zeros_like(sb_ref)

    row = lax.broadcasted_iota(jnp.int32, (C, C), 0)
    col = lax.broadcasted_iota(jnp.int32, (C, C), 1)
    dn_t = (((1,), (1,)), ((), ()))

    def one(b, d, q_ref, k_ref, v_ref, g_ref, o_ref, s_ref):
        tri = (row >= col) if d == 0 else (row <= col)
        trib = jnp.where(tri, 1.0, 0.0).astype(BF16)
        logits = jnp.dot(g_ref[b], gup_ref[d], preferred_element_type=F32) + gbias_ref[d]
        la = (jnp.minimum(logits, 0.0) - jnp.log(1.0 + jnp.exp(-jnp.abs(logits)))) * (1.0 / GLA_GATE_NORM)
        hi = la.astype(BF16)
        r1 = la - hi.astype(F32)
        mid = r1.astype(BF16)
        lo = (r1 - mid.astype(F32)).astype(BF16)
        bcum = (jnp.dot(trib, hi, preferred_element_type=F32) + jnp.dot(trib, mid, preferred_element_type=F32)
                + jnp.dot(trib, lo, preferred_element_type=F32))
        total = bcum[C - 1:C] if d == 0 else bcum[0:1]
        q = q_ref[b].astype(F32) * (GLA_DK ** -0.5)
        k = k_ref[b].astype(F32)
        qt = (q * jnp.exp(bcum)).astype(BF16)
        kt = (k * jnp.exp(-bcum)).astype(BF16)
        ks = (k * jnp.exp(total - bcum)).astype(BF16)
        et = jnp.exp(total)
        v = v_ref[b]
        for h in range(GLA_HEADS):
            sl = slice(h * GLA_DK, (h + 1) * GLA_DK)
            vs = v[:, h * GLA_DV:(h + 1) * GLA_DV]
            attn = lax.dot_general(qt[:, sl], kt[:, sl], dn_t, preferred_element_type=F32)
            attn = jnp.where(tri, attn, 0.0).astype(BF16)
            st = s_ref[b * GLA_HEADS + h]
            o = (jnp.dot(attn, vs, preferred_element_type=F32)
                 + lax.dot_general(qt[:, sl], st.astype(BF16), dn_t, preferred_element_type=F32))
            o_ref[b, :, h * GLA_DV:(h + 1) * GLA_DV] = o
            upd = lax.dot_general(vs, ks[:, sl], (((0,), (0,)), ((), ())), preferred_element_type=F32)
            s_ref[b * GLA_HEADS + h] = st * et[:, sl] + upd

    def body(b, carry):
        one(b, 0, qf, kf, vf, gf, of_ref, sf_ref)
        one(b, 1, qb, kb, vb, gb, ob_ref, sb_ref)
        return carry
    lax.fori_loop(0, batch, body, 0)


def _gla(p3, pg3, gup, gbias):
    B, S, _ = p3.shape
    C = GLA_CHUNK
    nch = S // C
    nctx = ROW_BLOCK // C
    bwd = lambda i: jnp.where(i < nctx, nctx - 1 - i, nch + nctx - 1 - i)
    wq, wv = GLA_HEADS * GLA_DK, GLA_HEADS * GLA_DV
    def specs(f):
        return [pl.BlockSpec((B, C, wq), lambda i: (0, f(i), COL_GQ // wq)),
                pl.BlockSpec((B, C, wq), lambda i: (0, f(i), COL_GK // wq)),
                pl.BlockSpec((B, C, wv), lambda i: (0, f(i), COL_GV // wv)),
                pl.BlockSpec((B, C, LANES), lambda i: (0, f(i), 0))]
    fwd = lambda i: i
    return pl.pallas_call(
        functools.partial(_gla_kernel, batch=B), grid=(nch,),
        in_specs=specs(fwd) + specs(bwd) + [pl.BlockSpec((2, LANES, wq), lambda i: (0, 0, 0)),
                                            pl.BlockSpec((2, 1, wq), lambda i: (0, 0, 0))],
        out_specs=[pl.BlockSpec((B, C, wv), lambda i: (0, i, 0)),
                   pl.BlockSpec((B, C, wv), lambda i: (0, bwd(i), 0))],
        out_shape=[jax.ShapeDtypeStruct((B, S, wv), F32)] * 2,
        scratch_shapes=[pltpu.VMEM((B * GLA_HEADS, GLA_DV, GLA_DK), F32)] * 2,
        compiler_params=_cparams(("arbitrary",)), name="gla",
    )(p3, p3, p3, pg3, p3, p3, p3, pg3, gup, gbias)


def _merge_kernel(oa_ref, ob_ref, of_ref, obk_ref, r_ref, g0_ref, g1_ref, g2_ref, gn_ref, wb_ref, y_ref):
    o = of_ref[...] + obk_ref[...]
    gn = gn_ref[...]
    parts = []
    for h in range(GLA_HEADS):
        oh = o[:, h * GLA_DV:(h + 1) * GLA_DV]
        ms = jnp.mean(oh * oh, axis=1, keepdims=True)
        parts.append(oh * lax.rsqrt(ms + LN_EPS) * gn)
    r = r_ref[...].astype(F32)
    oc = (jnp.concatenate(parts, axis=1) * (r * _sigmoid(r))).astype(BF16)
    y = (_sigmoid(g0_ref[...].astype(F32)) * jnp.dot(oa_ref[...], wb_ref[0], preferred_element_type=F32)
         + _sigmoid(g1_ref[...].astype(F32)) * jnp.dot(ob_ref[...], wb_ref[1], preferred_element_type=F32)
         + _sigmoid(g2_ref[...].astype(F32)) * jnp.dot(oc, wb_ref[2], preferred_element_type=F32))
    y_ref[...] = y.astype(y_ref.dtype)


def _merge(oa, ob, of, obk, p2, gn, wb):
    T = oa.shape[0]
    D = D_MODEL
    W = 1024
    tm = ROW_BLOCK
    row = lambda n: pl.BlockSpec((tm, n), lambda i: (i, 0))
    return pl.pallas_call(
        _merge_kernel, grid=(T // tm,),
        in_specs=[row(W), row(W), row(W), row(W),
                  pl.BlockSpec((tm, W), lambda i: (i, COL_R // W)),
                  pl.BlockSpec((tm, D), lambda i: (i, 0)),
                  pl.BlockSpec((tm, D), lambda i: (i, 1)),
                  pl.BlockSpec((tm, D), lambda i: (i, 2)),
                  pl.BlockSpec((1, GLA_DV), lambda i: (0, 0)),
                  pl.BlockSpec((3, W, D), lambda i: (0, 0, 0))],
        out_specs=row(D),
        out_shape=jax.ShapeDtypeStruct((T, D), BF16),
        compiler_params=_cparams(("parallel",)), name="merge",
    )(oa, ob, of, obk, p2, p2, p2, p2, gn, wb)


def _layer_norm(z, g, b):
    mu = jnp.mean(z, axis=1, keepdims=True)
    zc = z - mu
    var = jnp.mean(zc * zc, axis=1, keepdims=True)
    return zc * lax.rsqrt(var + LN_EPS) * g + b


def _outln_kernel(y_ref, w_ref, x_ref, g1_ref, sc_ref, sh_ref, lng_ref, lnb_ref, x_out, ht_out, *, alpha):
    mix = jnp.dot(y_ref[...], w_ref[...], preferred_element_type=F32)
    xn = _layer_norm(alpha * x_ref[...] + g1_ref[...] * mix, lng_ref[...], lnb_ref[...])
    x_out[...] = xn
    ht_out[...] = (xn * (1.0 + sc_ref[...]) + sh_ref[...]).T.astype(ht_out.dtype)


def _out_ln(y, w_out, x, mod_l, lng, lnb, nblk, ctx_row, alpha):
    T, D = x.shape
    tm = ROW_BLOCK
    row = pl.BlockSpec((tm, D), lambda i: (i, 0))
    vec = pl.BlockSpec((1, D), lambda i: (0, 0))
    return pl.pallas_call(
        functools.partial(_outln_kernel, alpha=alpha), grid=(T // tm,),
        in_specs=[row, pl.BlockSpec((D, D), lambda i: (0, 0)), row,
                  _mod_spec(2, nblk, ctx_row), _mod_spec(4, nblk, ctx_row), _mod_spec(3, nblk, ctx_row), vec, vec],
        out_specs=[row, pl.BlockSpec((D, tm), lambda i: (0, i))],
        out_shape=[jax.ShapeDtypeStruct((T, D), F32), jax.ShapeDtypeStruct((D, T), BF16)],
        compiler_params=_cparams(("parallel",)), name="out_proj_ln",
    )(y, w_out, x, mod_l, mod_l, mod_l, lng, lnb)


def _ln2_kernel(x_ref, ft_ref, g2_ref, lng_ref, lnb_ref, *rest, alpha, emit_h):
    xn = _layer_norm(alpha * x_ref[...] + g2_ref[...] * ft_ref[...].T, lng_ref[...], lnb_ref[...])
    if emit_h:
        sc_ref, sh_ref, x_out, h_out = rest
        h_out[...] = (xn * (1.0 + sc_ref[...]) + sh_ref[...]).astype(h_out.dtype)
    else:
        (x_out,) = rest
    x_out[...] = xn


def _ln2(x, ffn_t, mod_l, lng, lnb, mod_next, nblk, ctx_row, alpha):
    T, D = x.shape
    tm = ROW_BLOCK
    row = pl.BlockSpec((tm, D), lambda i: (i, 0))
    vec = pl.BlockSpec((1, D), lambda i: (0, 0))
    emit_h = mod_next is not None
    in_specs = [row, pl.BlockSpec((D, tm), lambda i: (0, i)), _mod_spec(5, nblk, ctx_row), vec, vec]
    args = [x, ffn_t, mod_l, lng, lnb]
    out_specs, out_shape = [row], [jax.ShapeDtypeStruct((T, D), F32)]
    if emit_h:
        in_specs += [_mod_spec(1, nblk, ctx_row), _mod_spec(0, nblk, ctx_row)]
        args += [mod_next, mod_next]
        out_specs.append(row)
        out_shape.append(jax.ShapeDtypeStruct((T, D), BF16))
    return pl.pallas_call(
        functools.partial(_ln2_kernel, alpha=alpha, emit_h=emit_h), grid=(T // tm,),
        in_specs=in_specs, out_specs=out_specs, out_shape=out_shape,
        compiler_params=_cparams(("parallel",)), name="ffn_ln",
    )(*args)


def _peer_scores_kernel(h_ref, wq_ref, k1_ref, k2_ref, s_ref):
    qt = jnp.dot(wq_ref[...], h_ref[...], preferred_element_type=F32).astype(BF16)
    n = PEER_NKEYS
    for h in range(PEER_HEADS):
        r = 2 * n * h
        s_ref[r:r + n, :] = jnp.dot(k1_ref[h], qt[r:r + n, :], preferred_element_type=F32)
        s_ref[r + n:r + 2 * n, :] = jnp.dot(k2_ref[h], qt[r + n:r + 2 * n, :], preferred_element_type=F32)


def _peer_scores(ht, wqt, k1, k2):
    D, T = ht.shape
    tt = _pick(T, (512, 256))
    Q = wqt.shape[0]
    n = PEER_NKEYS
    return pl.pallas_call(
        _peer_scores_kernel, grid=(T // tt,),
        in_specs=[pl.BlockSpec((D, tt), lambda i: (0, i)), pl.BlockSpec((Q, D), lambda i: (0, 0)),
                  pl.BlockSpec((PEER_HEADS, n, n), lambda i: (0, 0, 0)),
                  pl.BlockSpec((PEER_HEADS, n, n), lambda i: (0, 0, 0))],
        out_specs=pl.BlockSpec((Q, tt), lambda i: (0, i)),
        out_shape=jax.ShapeDtypeStruct((Q, T), F32),
        compiler_params=_cparams(("parallel",)), name="peer_scores",
    )(ht, wqt, k1, k2)


def _peer_select_kernel(s_ref, nt_ref, e1_ref, s2_ref, e2_ref):
    n, K = PEER_NKEYS, PEER_TOPK
    neg = -jnp.inf

    def top_values(s, count):
        vals = []
        for _ in range(count):
            m = jnp.max(s, axis=0, keepdims=True)
            vals.append(m)
            s = jnp.where(s == m, neg, s)
        return vals

    def body(h, carry):
        r1 = pl.multiple_of(h * 2 * n, 2 * n)
        r2 = pl.multiple_of(h * 2 * n + n, n)
        ro = pl.multiple_of(h * n, n)
        s1 = s_ref[pl.ds(r1, n), :]
        s2 = s_ref[pl.ds(r2, n), :]
        v1 = top_values(s1, K + 1)
        v2 = top_values(s2, K + 1)
        half = K // 2
        v1m = jnp.concatenate(v1[half:K], axis=0)
        v2m = jnp.concatenate(v2[:K], axis=0)
        cand = jnp.concatenate([v1[0] + v2m] + [v1[a] + v2m[:half] for a in range(1, half)]
                               + [v1m + v2[0]], axis=0)
        tops = top_values(cand, K + 1)
        nxt = jnp.maximum(tops[K], jnp.maximum(v1[K] + v2[0], v1[0] + v2[K]))
        tau = 0.5 * (tops[K - 1] + nxt)
        mtop = v1[0] + v2[0]
        z = jnp.sum(jnp.where(cand >= tau, jnp.exp(cand - mtop), 0.0), axis=0, keepdims=True)
        nt_ref[pl.ds(ro, n), :] = tau - s1
        e1_ref[pl.ds(ro, n), :] = jnp.exp(s1 - v1[0]) / z
        s2_ref[pl.ds(ro, n), :] = s2
        e2_ref[pl.ds(ro, n), :] = jnp.exp(s2 - v2[0])
        return carry
    lax.fori_loop(0, PEER_HEADS, body, 0)


def _peer_select(st):
    Q, T = st.shape
    tt = _pick(T, (256, 128))
    R = PEER_HEADS * PEER_NKEYS
    out = pl.BlockSpec((R, tt), lambda i: (0, i))
    return pl.pallas_call(
        _peer_select_kernel, grid=(T // tt,),
        in_specs=[pl.BlockSpec((Q, tt), lambda i: (0, i))],
        out_specs=[out] * 4,
        out_shape=[jax.ShapeDtypeStruct((R, T), F32)] * 4,
        compiler_params=_cparams(("parallel",)), name="peer_select",
    )(st)


def _peer_main_kernel(h_ref, u_ref, vt_ref, nt_ref, e1_ref, s2_ref, e2_ref, o_ref, act_a, act_b, p_a, p_b,
                      *, n_i1, n_e):
    e = pl.program_id(1)
    n = PEER_NKEYS

    @pl.when(e == 0)
    def _():
        o_ref[...] = jnp.zeros_like(o_ref)
        act_b[...] = jnp.zeros_like(act_b)
        p_a[...] = jnp.zeros_like(p_a)

    def step(act_w, act_r, p_w, p_r):
        o_ref[...] += jnp.dot(vt_ref[...], p_r[...], preferred_element_type=F32)

        tile = jnp.clip(e - 1, 0, n_e - 1)
        nr = 4
        rh = n // nr
        for c in range(h_ref.shape[1] // LANES):
            cs = slice(c * LANES, (c + 1) * LANES)
            for r in range(nr):
                acc = [None] * n_i1
                for h in range(PEER_HEADS):
                    rs = slice(h * n + r * rh, h * n + (r + 1) * rh)
                    s2 = s2_ref[rs, cs]
                    e2 = e2_ref[rs, cs]
                    for a in range(n_i1):
                        row = h * n + tile * n_i1 + a
                        ntr = nt_ref[pl.ds(row, 1), :][:, cs]
                        e1r = e1_ref[pl.ds(row, 1), :][:, cs]
                        g = jnp.where(s2 >= ntr, e1r * e2, 0.0)
                        acc[a] = g if acc[a] is None else acc[a] + g
                for a in range(n_i1):
                    ps = slice(a * n + r * rh, a * n + (r + 1) * rh)
                    p_w[ps, cs] = (_gelu(act_r[ps, cs]) * acc[a]).astype(p_w.dtype)

        act_w[...] = jnp.dot(u_ref[...], h_ref[...], preferred_element_type=F32)

    @pl.when(e % 2 == 0)
    def _():
        step(act_a, act_b, p_b, p_a)

    @pl.when(e % 2 == 1)
    def _():
        step(act_b, act_a, p_a, p_b)


def _peer_main(ht, ub, vtb, nt, e1, s2, e2):
    D, T = ht.shape
    E = ub.shape[0]
    tb = _pick(T, (512, 256))
    te = 512
    n_e = E // te
    R = PEER_HEADS * PEER_NKEYS
    sel = pl.BlockSpec((R, tb), lambda i, e: (0, i))
    return pl.pallas_call(
        functools.partial(_peer_main_kernel, n_i1=te // PEER_NKEYS, n_e=n_e), grid=(T // tb, n_e + 2),
        in_specs=[pl.BlockSpec((D, tb), lambda i, e: (0, i)),
                  pl.BlockSpec((te, D), lambda i, e: (jnp.minimum(e, n_e - 1), 0)),
                  pl.BlockSpec((D, te), lambda i, e: (0, jnp.clip(e - 2, 0, n_e - 1))),
                  sel, sel, sel, sel],
        out_specs=pl.BlockSpec((D, tb), lambda i, e: (0, i)),
        out_shape=jax.ShapeDtypeStruct((D, T), F32),
        scratch_shapes=[pltpu.VMEM((te, tb), F32), pltpu.VMEM((te, tb), F32),
                        pltpu.VMEM((te, tb), BF16), pltpu.VMEM((te, tb), BF16)],
        compiler_params=_cparams(("parallel", "arbitrary")), name="peer_main",
    )(ht, ub, vtb, nt, e1, s2, e2)


def _peer(ht, wq, k1, k2, u_tab, v_tab):
    st = _peer_scores(ht, wq.T.astype(BF16), k1.astype(BF16), k2.astype(BF16))
    nt, e1, s2, e2 = _peer_select(st)
    return _peer_main(ht, u_tab.astype(BF16), v_tab.T.astype(BF16), nt, e1, s2, e2)


def kernel(x, c, ctx, c_ctx, w_ada, b_ada, w_in, lam_q1, lam_k1, lam_q2, lam_k2, da_norm_g, s5_a_re, s5_a_im, s5_log_step, s5_b_re, s5_b_im, s5_c_re, s5_c_im, s5_d, s5_glu_w, s5_glu_b, gla_gate_up, gla_gate_b, gla_norm_g, w_branch, w_out, ln1_g, ln1_b, ln2_g, ln2_b, peer_wq, peer_k1, peer_k2, peer_u, peer_v):
    B, SEQ, D = x.shape
    CTX = ctx.shape[1]
    L = w_ada.shape[0]
    assert CTX == ROW_BLOCK and SEQ % ROW_BLOCK == 0 and D == D_MODEL and B < 8
    S = CTX + SEQ
    T = B * S
    nblk = S // ROW_BLOCK
    ctx_row = B
    alpha = (2 * L) ** 0.25

    c8 = jnp.zeros((8, D), F32).at[:B].set(c.astype(F32)).at[B].set(c_ctx.astype(F32))
    mod = _ada_mod(c8, w_ada, b_ada).reshape(L, 8, 6, 1, D)
    tables = _rope_tables(CTX, SEQ)
    xs = jnp.concatenate([ctx, x], axis=1).reshape(T, D).astype(F32)

    sizes = np.cumsum([0, 1024, 1024, 1024, 1024, 512, 512, 1024, 16, 16, 1024, 6144])
    seg = lambda w, k: w[:, sizes[k]:sizes[k + 1]]

    for l in range(L):
        lam_init = 0.8 - 0.6 * math.exp(-0.3 * l)
        mod_l = mod[l]
        wl = w_in[l]
        w_main = jnp.concatenate([seg(wl, 10), seg(wl, 0), seg(wl, 1), seg(wl, 2), seg(wl, 3), seg(wl, 4),
                                  seg(wl, 5), seg(wl, 6), seg(wl, 9)], axis=1).astype(BF16)
        w_gate = jnp.concatenate([seg(wl, 7), seg(wl, 8), jnp.zeros((D, LANES - 2 * GLA_RANK), wl.dtype)],
                                 axis=1).astype(BF16)
        if l == 0:
            h = _modulate(xs, mod_l, nblk, ctx_row)
        p2 = _matmul(h, w_main, BF16, "in_proj")
        pg = _matmul(h, w_gate, BF16, "in_proj_gates")
        p3 = p2.reshape(B, S, N_MAIN)

        lamv = jnp.zeros((8, LANES), F32)
        for r, t in enumerate((lam_q1[l], lam_k1[l], lam_q2[l], lam_k2[l])):
            lamv = lamv.at[r, :DA_QK_DIM].set(t.astype(F32))
        oa = _diff_attention(p3, lamv, da_norm_g[l].astype(F32).reshape(1, DA_V_DIM), tables, lam_init)

        prm = _s5_params(s5_a_re[l], s5_a_im[l], s5_log_step[l], s5_b_re[l], s5_b_im[l],
                         s5_c_re[l], s5_c_im[l], s5_d[l], B)
        ob = _s5_mixer(p3, prm, s5_glu_w[l], s5_glu_b[l])

        gup = jnp.zeros((2, LANES, GLA_HEADS * GLA_DK), F32)
        gup = gup.at[0, :GLA_RANK].set(gla_gate_up[l, 0]).at[1, GLA_RANK:2 * GLA_RANK].set(gla_gate_up[l, 1])
        gbias = gla_gate_b[l].astype(F32).reshape(2, 1, GLA_HEADS * GLA_DK)
        of, obk = _gla(p3, pg.reshape(B, S, LANES), gup.astype(BF16), gbias)

        y = _merge(oa.reshape(T, -1), ob, of.reshape(T, -1), obk.reshape(T, -1), p2,
                   gla_norm_g[l].astype(F32).reshape(1, GLA_DV), w_branch[l].astype(BF16))
        x1, h2t = _out_ln(y, w_out[l].astype(BF16), xs, mod_l, ln1_g[l].reshape(1, D), ln1_b[l].reshape(1, D),
                          nblk, ctx_row, alpha)
        ffn_t = _peer(h2t, peer_wq[l], peer_k1[l], peer_k2[l], peer_u[l], peer_v[l])
        res = _ln2(x1, ffn_t, mod_l, ln2_g[l].reshape(1, D), ln2_b[l].reshape(1, D),
                   mod[l + 1] if l + 1 < L else None, nblk, ctx_row, alpha)
        xs = res[0]
        if l + 1 < L:
            h = res[1]

    return xs.reshape(B, S, D)[:, CTX:, :]
```

```python
import functools
import math

import jax
import jax.numpy as jnp
import numpy as np
from jax import lax
from jax.experimental import pallas as pl
from jax.experimental.pallas import tpu as pltpu

F32 = jnp.float32
BF16 = jnp.bfloat16

D_MODEL = 2048
GRID_W = 64
DA_HEADS = 8
DA_QK_DIM = 64
DA_V_DIM = 128
AX_DIM = DA_QK_DIM // 2
ROPE_BASE = 10000.0
S5_WIDTH = 1024
S5_GROUP = 16
S5_GROUPS = 64
S5_STATE = 64
GLA_HEADS = 4
GLA_DK = 128
GLA_DV = 256
GLA_RANK = 16
GLA_CHUNK = 64
GLA_GATE_NORM = 16.0
PEER_HEADS = 8
PEER_NKEYS = 128
PEER_TOPK = 16
LN_EPS = 1e-5

LANES = 128
ROW_BLOCK = 256
S5_CHUNK = 16
VMEM_LIMIT = 56 * 1024 * 1024

COL_MG = 0
COL_Q = 6144
COL_K = 7168
COL_V = 8192
COL_U = 9216
COL_GQ = 10240
COL_GK = 10752
COL_GV = 11264
COL_R = 12288
N_MAIN = 13312


def _cparams(sem):
    return pltpu.CompilerParams(dimension_semantics=sem, vmem_limit_bytes=VMEM_LIMIT)


def _pick(n, cands):
    for c in cands:
        if n % c == 0:
            return c
    raise ValueError(f"no tile for {n}")


def _gelu(x):
    return 0.5 * x * (1.0 + jnp.tanh(math.sqrt(2.0 / math.pi) * (x + 0.044715 * (x * x * x))))


def _sigmoid(x):
    return 1.0 / (1.0 + jnp.exp(-x))


def _ada_kernel(c_ref, w_ref, b_ref, o_ref):
    c = c_ref[...]
    s = (c * _sigmoid(c)).astype(BF16)
    o_ref[...] = jnp.dot(s, w_ref[...].astype(BF16), preferred_element_type=F32) + b_ref[...]


def _ada_mod(c8, w_ada, b_ada):
    L, D, N = w_ada.shape
    tn = 1536
    return pl.pallas_call(
        _ada_kernel,
        grid=(L, N // tn),
        in_specs=[pl.BlockSpec((8, D), lambda l, j: (0, 0)),
                  pl.BlockSpec((None, D, tn), lambda l, j: (l, 0, j)),
                  pl.BlockSpec((None, 1, tn), lambda l, j: (l, 0, j))],
        out_specs=pl.BlockSpec((None, 8, tn), lambda l, j: (l, 0, j)),
        out_shape=jax.ShapeDtypeStruct((L, 8, N), F32),
        compiler_params=_cparams(("parallel", "parallel")),
        name="ada_mod",
    )(c8, w_ada, b_ada.reshape(L, 1, N))


def _mod_spec(k, nblk, ctx_row):
    return pl.BlockSpec((None, None, 1, D_MODEL),
                        lambda i: (jnp.where(i % nblk == 0, ctx_row, i // nblk), k, 0, 0))


def _modulate_kernel(x_ref, sc_ref, sh_ref, o_ref):
    o_ref[...] = (x_ref[...] * (1.0 + sc_ref[...]) + sh_ref[...]).astype(o_ref.dtype)


def _modulate(x, mod_l, nblk, ctx_row):
    T, D = x.shape
    return pl.pallas_call(
        _modulate_kernel,
        grid=(T // ROW_BLOCK,),
        in_specs=[pl.BlockSpec((ROW_BLOCK, D), lambda i: (i, 0)),
                  _mod_spec(1, nblk, ctx_row), _mod_spec(0, nblk, ctx_row)],
        out_specs=pl.BlockSpec((ROW_BLOCK, D), lambda i: (i, 0)),
        out_shape=jax.ShapeDtypeStruct((T, D), BF16),
        compiler_params=_cparams(("parallel",)),
        name="modulate",
    )(x, mod_l, mod_l)


def _mm_kernel(a_ref, b_ref, o_ref):
    o_ref[...] = jnp.dot(a_ref[...], b_ref[...], preferred_element_type=F32).astype(o_ref.dtype)


def _matmul(a, b, out_dtype, name):
    M, K = a.shape
    N = b.shape[1]
    tm = _pick(M, (1024, 768, 512, 256))
    tn = _pick(N, (1024, 512, 256, 128))
    return pl.pallas_call(
        _mm_kernel,
        grid=(M // tm, N // tn),
        in_specs=[pl.BlockSpec((tm, K), lambda i, j: (i, 0)),
                  pl.BlockSpec((K, tn), lambda i, j: (0, j))],
        out_specs=pl.BlockSpec((tm, tn), lambda i, j: (i, j)),
        out_shape=jax.ShapeDtypeStruct((M, N), out_dtype),
        compiler_params=_cparams(("parallel", "parallel")),
        name=name,
    )(a, b)


def _attn_kernel(lam_ref, g_ref, q_ref, k_ref, v_ref, cos_ref, sa_ref, sb_ref, o_ref, kr_ref, ve_ref,
                 *, lam_init, n_ctx, tk, n_lat, tq):
    qi = pl.program_id(2)
    n_rows = n_ctx + n_lat * tk

    def rope(x, r0, n):
        c = cos_ref[pl.ds(r0, n), :]
        sa = sa_ref[pl.ds(r0, n), :]
        sb = sb_ref[pl.ds(r0, n), :]
        return x * c + pltpu.roll(x, 16, 1) * sa + pltpu.roll(x, LANES - 16, 1) * sb

    @pl.when(qi == 0)
    def _():
        def body(j, carry):
            r0 = pl.multiple_of(j * tq, tq)
            kr_ref[pl.ds(r0, tq), :] = rope(k_ref[pl.ds(r0, tq), :].astype(F32), r0, tq).astype(BF16)
            ve_ref[pl.ds(r0, tq), 0:DA_V_DIM] = v_ref[pl.ds(r0, tq), :]
            ve_ref[pl.ds(r0, tq), DA_V_DIM:2 * DA_V_DIM] = jnp.ones((tq, DA_V_DIM), BF16)
            return carry
        lax.fori_loop(0, n_rows // tq, body, 0)

    lv = lam_ref[...]
    lam = (jnp.exp(jnp.sum(lv[0:1] * lv[1:2], axis=1, keepdims=True))
           - jnp.exp(jnp.sum(lv[2:3] * lv[3:4], axis=1, keepdims=True)) + lam_init)

    q = rope(q_ref[...].astype(F32), pl.multiple_of(qi * tq, tq), tq) * (DA_QK_DIM ** -0.5 * math.log2(math.e))
    lane = lax.broadcasted_iota(jnp.int32, q.shape, 1)
    qs = (jnp.where(lane < DA_QK_DIM, q, 0.0).astype(BF16), jnp.where(lane >= DA_QK_DIM, q, 0.0).astype(BF16))

    def chunk(carry, r0, n):
        kj = kr_ref[r0:r0 + n, :]
        vj = ve_ref[r0:r0 + n, :]
        out = []
        for c in range(2):
            s = lax.dot_general(qs[c], kj, (((1,), (1,)), ((), ())), preferred_element_type=F32)
            mx = jnp.max(s, axis=1, keepdims=True)
            if carry is None:
                mn = mx
                acc = jnp.dot(jnp.exp2(s - mn).astype(BF16), vj, preferred_element_type=F32)
            else:
                m, acc = carry[c]
                mn = jnp.maximum(m, mx)
                acc = jnp.exp2(m - mn) * acc + jnp.dot(jnp.exp2(s - mn).astype(BF16), vj,
                                                      preferred_element_type=F32)
            out.append((mn, acc))
        return out

    def finish(carry):
        (_, acc_a), (_, acc_b) = carry
        o = (acc_a[:, :DA_V_DIM] / acc_a[:, DA_V_DIM:] - lam * (acc_b[:, :DA_V_DIM] / acc_b[:, DA_V_DIM:]))
        ms = jnp.mean(o * o, axis=1, keepdims=True)
        o = o * lax.rsqrt(ms + LN_EPS) * g_ref[...] * (1.0 - lam_init)
        o_ref[...] = o.astype(o_ref.dtype)

    @pl.when(qi == 0)
    def _():
        finish(chunk(None, 0, n_ctx))

    @pl.when(qi > 0)
    def _():
        carry = chunk(None, 0, n_ctx)
        for j in range(n_lat):
            carry = chunk(carry, n_ctx + j * tk, tk)
        finish(carry)


def _rope_tables(ctx_len, seq):
    n = np.arange(seq)
    pos = np.stack([n // GRID_W, n % GRID_W], axis=0).astype(np.float32)
    inv = jnp.asarray(ROPE_BASE, F32) ** (-jnp.arange(AX_DIM // 2, dtype=F32) * 2.0 / AX_DIM)
    lane = np.arange(LANES)
    ax = (lane % DA_QK_DIM) // AX_DIM
    half = (lane % AX_DIM) // (AX_DIM // 2)
    j = lane % (AX_DIM // 2)
    ang = jnp.asarray(pos)[ax, :].T * inv[j][None, :]
    cos, sin = jnp.cos(ang), jnp.sin(ang)
    sa = jnp.where(half[None, :] == 1, sin, 0.0)
    sb = jnp.where(half[None, :] == 0, -sin, 0.0)
    pad = lambda t, v: jnp.concatenate([jnp.full((ctx_len, LANES), v, F32), t.astype(F32)], axis=0)
    return pad(cos, 1.0), pad(sa, 0.0), pad(sb, 0.0)


def _diff_attention(p3, lamv, norm_g, tables, lam_init):
    B, S, _ = p3.shape
    tq = ROW_BLOCK
    n_ctx = ROW_BLOCK
    tk = _pick(S - n_ctx, (1024, 512, 256))
    cq, ck, cv = COL_Q // LANES, COL_K // LANES, COL_V // LANES
    tab_spec = pl.BlockSpec((S, LANES), lambda b, h, i: (0, 0))
    return pl.pallas_call(
        functools.partial(_attn_kernel, lam_init=lam_init, n_ctx=n_ctx, tk=tk, n_lat=(S - n_ctx) // tk, tq=tq),
        grid=(B, DA_HEADS, S // tq),
        in_specs=[pl.BlockSpec((8, LANES), lambda b, h, i: (0, 0)),
                  pl.BlockSpec((1, LANES), lambda b, h, i: (0, 0)),
                  pl.BlockSpec((None, tq, LANES), lambda b, h, i: (b, i, cq + h)),
                  pl.BlockSpec((None, S, LANES), lambda b, h, i: (b, 0, ck + h)),
                  pl.BlockSpec((None, S, LANES), lambda b, h, i: (b, 0, cv + h)),
                  tab_spec, tab_spec, tab_spec],
        out_specs=pl.BlockSpec((None, tq, LANES), lambda b, h, i: (b, i, h)),
        out_shape=jax.ShapeDtypeStruct((B, S, DA_HEADS * DA_V_DIM), BF16),
        scratch_shapes=[pltpu.VMEM((S, LANES), BF16), pltpu.VMEM((S, 2 * DA_V_DIM), BF16)],
        compiler_params=_cparams(("parallel", "parallel", "arbitrary")),
        name="diff_attention",
    )(lamv, norm_g, p3, p3, p3, *tables)


def _s5_params(a_re, a_im, log_step, b_re, b_im, c_re, c_im, d, batch):
    C = S5_CHUNK
    G, P, I = S5_GROUPS, S5_STATE, S5_GROUP
    hp = lax.Precision.HIGHEST
    dt = jnp.exp(log_step.astype(F32))[:, :, None]
    ar, ai = a_re.astype(F32), a_im.astype(F32)
    mag = jnp.exp(ar * dt)
    l_re, l_im = mag * jnp.cos(ai * dt), mag * jnp.sin(ai * dt)
    nr, ni = l_re - 1.0, l_im
    den = ar * ar + ai * ai
    f_re = (nr * ar + ni * ai) / den
    f_im = (ni * ar - nr * ai) / den
    br, bi = b_re.astype(F32), b_im.astype(F32)
    bb_re = f_re[..., None] * br - f_im[..., None] * bi
    bb_im = f_re[..., None] * bi + f_im[..., None] * br
    cr, ci = c_re.astype(F32), c_im.astype(F32)
    kpow = jnp.arange(C + 1, dtype=F32)[None, :, None, None]
    pmag = jnp.exp(kpow * (ar * dt)[:, None])
    pw_re = pmag * jnp.cos(kpow * (ai * dt)[:, None])
    pw_im = pmag * jnp.sin(kpow * (ai * dt)[:, None])
    cp_re = cr[:, None] * pw_re[:, :C, :, None, :] - ci[:, None] * pw_im[:, :C, :, None, :]
    cp_im = cr[:, None] * pw_im[:, :C, :, None, :] + ci[:, None] * pw_re[:, :C, :, None, :]
    kk = (jnp.einsum('dlgop,dgpi->dlgoi', cp_re, bb_re, precision=hp)
          - jnp.einsum('dlgop,dgpi->dlgoi', cp_im, bb_im, precision=hp))
    lag = np.arange(C)[:, None, None]
    s_idx = np.arange(C)[None, :, None]
    j_idx = np.arange(C)[None, None, :]
    sel_f = jnp.asarray((j_idx - s_idx == lag).astype(np.float32))
    sel_b = jnp.asarray((s_idx - j_idx == lag).astype(np.float32))
    mf = jnp.einsum('lsj,lgoi->sjgoi', sel_f, kk[0], precision=hp)
    mb = jnp.einsum('lsj,lgoi->sjgoi', sel_b, kk[1], precision=hp)
    m = (mf + mb).transpose(2, 0, 4, 1, 3).reshape(G, C * I, C * I)
    def wmat(d, pick):
        pr, pi = pick(pw_re[d])[..., None], pick(pw_im[d])[..., None]
        w_re = pr * bb_re[d][None] - pi * bb_im[d][None]
        w_im = pr * bb_im[d][None] + pi * bb_re[d][None]
        t = lambda w: w.transpose(1, 0, 3, 2).reshape(G, C * I, P)
        return jnp.concatenate([t(w_re), t(w_im)], axis=-1)
    w = jnp.concatenate([wmat(0, lambda t: t[:C][::-1]), wmat(1, lambda t: t[:C])], axis=-1)
    def vmat(d, pick):
        pr, pi = pick(pw_re[d])[:, :, None, :], pick(pw_im[d])[:, :, None, :]
        v_re = cr[d][None] * pr - ci[d][None] * pi
        v_im = cr[d][None] * pi + ci[d][None] * pr
        t = lambda v: v.transpose(1, 3, 0, 2).reshape(G, P, C * I)
        return jnp.concatenate([t(v_re), -t(v_im)], axis=1)
    v = jnp.concatenate([vmat(0, lambda t: t[1:]), vmat(1, lambda t: t[1:][::-1])], axis=1)
    lc_re, lc_im = pw_re[:, C], pw_im[:, C]
    def rows(t):
        return jnp.repeat(t, batch, axis=0)
    ca = jnp.stack([rows(jnp.concatenate([lc_re[k], lc_re[k]], -1)) for k in range(2)])
    cb = jnp.stack([rows(jnp.concatenate([-lc_im[k], lc_im[k]], -1)) for k in range(2)])
    dflat = jnp.tile(d.astype(F32).reshape(G, 1, I), (1, C, 1)).reshape(G, 1, C * I)
    return m.astype(BF16), w.astype(BF16), v.astype(BF16), ca, cb, dflat


S5_GBLK = LANES // S5_GROUP


def _s5_lane_perm():
    C, I = S5_CHUNK, S5_GROUP
    r = np.arange(C * LANES)
    s_, g_, i_ = r // LANES, (r % LANES) // I, r % I
    dst = g_ * (C * I) + s_ * I + i_
    return (jnp.asarray(dst)[:, None] == jnp.arange(C * LANES)[None, :]).astype(BF16)


def _s5_sum_kernel(x_ref, perm_ref, w_ref, s_ref, ug_ref):
    K = S5_CHUNK * S5_GROUP
    up = jnp.dot(x_ref[...], perm_ref[...], preferred_element_type=F32).astype(ug_ref.dtype)
    for g in range(S5_GBLK):
        ug = up[:, g * K:(g + 1) * K]
        ug_ref[g] = ug
        s_ref[g] = jnp.dot(ug, w_ref[g], preferred_element_type=F32)


def _s5_summaries(x, perm, w):
    NB, R, KX = x.shape
    K = w.shape[1]
    tr = R // 2
    grp = lambda n: pl.BlockSpec((S5_GBLK, n, K), lambda i, j: (i, 0, 0))
    out = pl.BlockSpec((S5_GBLK, tr, K), lambda i, j: (i, j, 0))
    return pl.pallas_call(
        _s5_sum_kernel, grid=(NB, 2),
        in_specs=[pl.BlockSpec((None, tr, KX), lambda i, j: (i, j, 0)),
                  pl.BlockSpec((KX, KX), lambda i, j: (0, 0)), grp(K)],
        out_specs=[out, out],
        out_shape=[jax.ShapeDtypeStruct((NB * S5_GBLK, R, K), F32), jax.ShapeDtypeStruct((NB * S5_GBLK, R, K), BF16)],
        compiler_params=_cparams(("parallel", "parallel")), name="s5_summaries",
    )(x, perm, w)


def _s5_scan_kernel(sf_ref, sb_ref, ca_ref, cb_ref, xf_out, xb_out, xf_ref, xb_ref, *, nch):
    i = pl.program_id(0)

    @pl.when(i == 0)
    def _():
        xf_ref[...] = jnp.zeros_like(xf_ref)
        xb_ref[...] = jnp.zeros_like(xb_ref)

    caf, cbf, cab, cbb = ca_ref[0], cb_ref[0], ca_ref[1], cb_ref[1]
    xf = xf_ref[...]
    xb = xb_ref[...]
    for j in range(nch):
        xf_out[j] = xf.astype(BF16)
        xf = xf * caf + pltpu.roll(xf, S5_STATE, 1) * cbf + sf_ref[j]
        jb = nch - 1 - j
        xb_out[jb] = xb.astype(BF16)
        xb = xb * cab + pltpu.roll(xb, S5_STATE, 1) * cbb + sb_ref[jb]
    xf_ref[...] = xf
    xb_ref[...] = xb


def _s5_scan(st, ca, cb):
    NC, R, _ = st.shape
    nch = ROW_BLOCK // S5_CHUNK
    nblk = NC // nch
    bwd = lambda i: jnp.where(i == 0, 0, nblk - i)
    P2 = 2 * S5_STATE
    return pl.pallas_call(
        functools.partial(_s5_scan_kernel, nch=nch), grid=(nblk,),
        in_specs=[pl.BlockSpec((nch, R, P2), lambda i: (i, 0, 0)),
                  pl.BlockSpec((nch, R, P2), lambda i: (bwd(i), 0, 1)),
                  pl.BlockSpec((2, R, P2), lambda i: (0, 0, 0)),
                  pl.BlockSpec((2, R, P2), lambda i: (0, 0, 0))],
        out_specs=[pl.BlockSpec((nch, R, P2), lambda i: (i, 0, 0)),
                   pl.BlockSpec((nch, R, P2), lambda i: (bwd(i), 0, 0))],
        out_shape=[jax.ShapeDtypeStruct((NC, R, P2), BF16)] * 2,
        scratch_shapes=[pltpu.VMEM((R, P2), F32), pltpu.VMEM((R, P2), F32)],
        compiler_params=_cparams(("arbitrary",)), name="s5_scan",
    )(st, st, ca, cb)


def _s5_out_kernel(u_ref, xp_ref, m_ref, v_ref, d_ref, perm_ref, o_ref):
    zs = []
    for g in range(S5_GBLK):
        u = u_ref[g]
        y = (jnp.dot(u, m_ref[g], preferred_element_type=F32)
             + jnp.dot(xp_ref[g], v_ref[g], preferred_element_type=F32)
             + u.astype(F32) * d_ref[g])
        zs.append(_gelu(y).astype(o_ref.dtype))
    z = jnp.concatenate(zs, axis=1)
    o_ref[...] = jnp.dot(z, perm_ref[...], preferred_element_type=F32).astype(o_ref.dtype)


def _s5_output(ug, xp, m, v, dflat, perm_t):
    G, R, K = ug.shape
    KX = perm_t.shape[0]
    tr = R // 2
    rows = lambda: pl.BlockSpec((S5_GBLK, tr, K), lambda i, j: (i, j, 0))
    grp = lambda n: pl.BlockSpec((S5_GBLK, n, K), lambda i, j: (i, 0, 0))
    return pl.pallas_call(
        _s5_out_kernel, grid=(G // S5_GBLK, 2),
        in_specs=[rows(), rows(), grp(K), grp(K), grp(1), pl.BlockSpec((KX, KX), lambda i, j: (0, 0))],
        out_specs=pl.BlockSpec((None, tr, KX), lambda i, j: (i, j, 0)),
        out_shape=jax.ShapeDtypeStruct((G // S5_GBLK, R, KX), BF16),
        compiler_params=_cparams(("parallel", "parallel")), name="s5_output",
    )(ug, xp, m, v, dflat, perm_t)


def _glu_kernel(z_ref, w_ref, b_ref, o_ref):
    z = z_ref[...]
    t = jnp.dot(z, w_ref[...], preferred_element_type=F32) + b_ref[...]
    o_ref[...] = (z.astype(F32) * _sigmoid(t)).astype(o_ref.dtype)


def _glu(z, w, b):
    T, N = z.shape
    tm = _pick(T, (1024, 768, 512, 256))
    return pl.pallas_call(
        _glu_kernel, grid=(T // tm,),
        in_specs=[pl.BlockSpec((tm, N), lambda i: (i, 0)), pl.BlockSpec((N, N), lambda i: (0, 0)),
                  pl.BlockSpec((1, N), lambda i: (0, 0))],
        out_specs=pl.BlockSpec((tm, N), lambda i: (i, 0)),
        out_shape=jax.ShapeDtypeStruct((T, N), BF16),
        compiler_params=_cparams(("parallel",)), name="s5_glu",
    )(z, w, b)


def _s5_mixer(p3, prm, glu_w, glu_b):
    B, S, _ = p3.shape
    m, w, v, ca, cb, dflat = prm
    C, G, I = S5_CHUNK, S5_GROUPS, S5_GROUP
    nc = S // C
    u = p3[:, :, COL_U:COL_U + S5_WIDTH]
    nb = S5_WIDTH // LANES
    x = u.reshape(B * nc, C, nb, LANES).transpose(2, 0, 1, 3).reshape(nb, B * nc, C * LANES)
    perm = _s5_lane_perm()
    s, ug = _s5_summaries(x, perm, w)
    st = s.reshape(G, B, nc, 4 * S5_STATE).transpose(2, 0, 1, 3).reshape(nc, G * B, 4 * S5_STATE)
    xf, xb = _s5_scan(st, ca, cb)
    xp = jnp.concatenate([xf, xb], axis=-1).reshape(nc, G, B, 4 * S5_STATE)
    xp = xp.transpose(1, 2, 0, 3).reshape(G, B * nc, 4 * S5_STATE)
    z = _s5_output(ug, xp, m, v, dflat, perm.T)
    z = z.reshape(nb, B * nc, C, LANES).transpose(1, 2, 0, 3).reshape(B * S, S5_WIDTH)
    return _glu(z, glu_w.astype(BF16), glu_b.astype(F32).reshape(1, S5_WIDTH))


def _gla_kernel(qf, kf, vf, gf, qb, kb, vb, gb, gup_ref, gbias_ref, of_ref, ob_ref, sf_ref, sb_ref, *, batch):
    i = pl.program_id(0)
    C = GLA_CHUNK

    @pl.when(i == 0)
    def _():
        sf_ref[...] = jnp.zeros_like(sf_ref)
        sb_ref[...] = jnp.zeros_like(sb_ref)

    row = lax.broadcasted_iota(jnp.int32, (C, C), 0)
    col = lax.broadcasted_iota(jnp.int32, (C, C), 1)
    dn_t = (((1,), (1,)), ((), ()))

    def one(b, d, q_ref, k_ref, v_ref, g_ref, o_ref, s_ref):
        tri = (row >= col) if d == 0 else (row <= col)
        trib = jnp.where(tri, 1.0, 0.0).astype(BF16)
        logits = jnp.dot(g_ref[b], gup_ref[d], preferred_element_type=F32) + gbias_ref[d]
        la = (jnp.minimum(logits, 0.0) - jnp.log(1.0 + jnp.exp(-jnp.abs(logits)))) * (1.0 / GLA_GATE_NORM)
        hi = la.astype(BF16)
        r1 = la - hi.astype(F32)
        mid = r1.astype(BF16)
        lo = (r1 - mid.astype(F32)).astype(BF16)
        bcum = (jnp.dot(trib, hi, preferred_element_type=F32) + jnp.dot(trib, mid, preferred_element_type=F32)
                + jnp.dot(trib, lo, preferred_element_type=F32))
        total = bcum[C - 1:C] if d == 0 else bcum[0:1]
        q = q_ref[b].astype(F32) * (GLA_DK ** -0.5)
        k = k_ref[b].astype(F32)
        qt = (q * jnp.exp(bcum)).astype(BF16)
        kt = (k * jnp.exp(-bcum)).astype(BF16)
        ks = (k * jnp.exp(total - bcum)).astype(BF16)
        et = jnp.exp(total)
        v = v_ref[b]
        for h in range(GLA_HEADS):
            sl = slice(h * GLA_DK, (h + 1) * GLA_DK)
            vs = v[:, h * GLA_DV:(h + 1) * GLA_DV]
            attn = lax.dot_general(qt[:, sl], kt[:, sl], dn_t, preferred_element_type=F32)
            attn = jnp.where(tri, attn, 0.0).astype(BF16)
            st = s_ref[b * GLA_HEADS + h]
            o = (jnp.dot(attn, vs, preferred_element_type=F32)
                 + lax.dot_general(qt[:, sl], st.astype(BF16), dn_t, preferred_element_type=F32))
            o_ref[b, :, h * GLA_DV:(h + 1) * GLA_DV] = o
            upd = lax.dot_general(vs, ks[:, sl], (((0,), (0,)), ((), ())), preferred_element_type=F32)
            s_ref[b * GLA_HEADS + h] = st * et[:, sl] + upd

    def body(b, carry):
        one(b, 0, qf, kf, vf, gf, of_ref, sf_ref)
        one(b, 1, qb, kb, vb, gb, ob_ref, sb_ref)
        return carry
    lax.fori_loop(0, batch, body, 0)


def _gla(p3, pg3, gup, gbias):
    B, S, _ = p3.shape
    C = GLA_CHUNK
    nch = S // C
    nctx = ROW_BLOCK // C
    bwd = lambda i: jnp.where(i < nctx, nctx - 1 - i, nch + nctx - 1 - i)
    wq, wv = GLA_HEADS * GLA_DK, GLA_HEADS * GLA_DV
    def specs(f):
        return [pl.BlockSpec((B, C, wq), lambda i: (0, f(i), COL_GQ // wq)),
                pl.BlockSpec((B, C, wq), lambda i: (0, f(i), COL_GK // wq)),
                pl.BlockSpec((B, C, wv), lambda i: (0, f(i), COL_GV // wv)),
                pl.BlockSpec((B, C, LANES), lambda i: (0, f(i), 0))]
    fwd = lambda i: i
    return pl.pallas_call(
        functools.partial(_gla_kernel, batch=B), grid=(nch,),
        in_specs=specs(fwd) + specs(bwd) + [pl.BlockSpec((2, LANES, wq), lambda i: (0, 0, 0)),
                                            pl.BlockSpec((2, 1, wq), lambda i: (0, 0, 0))],
        out_specs=[pl.BlockSpec((B, C, wv), lambda i: (0, i, 0)),
                   pl.BlockSpec((B, C, wv), lambda i: (0, bwd(i), 0))],
        out_shape=[jax.ShapeDtypeStruct((B, S, wv), F32)] * 2,
        scratch_shapes=[pltpu.VMEM((B * GLA_HEADS, GLA_DV, GLA_DK), F32)] * 2,
        compiler_params=_cparams(("arbitrary",)), name="gla",
    )(p3, p3, p3, pg3, p3, p3, p3, pg3, gup, gbias)


def _merge_kernel(oa_ref, ob_ref, of_ref, obk_ref, r_ref, g0_ref, g1_ref, g2_ref, gn_ref, wb_ref, y_ref):
    o = of_ref[...] + obk_ref[...]
    gn = gn_ref[...]
    parts = []
    for h in range(GLA_HEADS):
        oh = o[:, h * GLA_DV:(h + 1) * GLA_DV]
        ms = jnp.mean(oh * oh, axis=1, keepdims=True)
        parts.append(oh * lax.rsqrt(ms + LN_EPS) * gn)
    r = r_ref[...].astype(F32)
    oc = (jnp.concatenate(parts, axis=1) * (r * _sigmoid(r))).astype(BF16)
    y = (_sigmoid(g0_ref[...].astype(F32)) * jnp.dot(oa_ref[...], wb_ref[0], preferred_element_type=F32)
         + _sigmoid(g1_ref[...].astype(F32)) * jnp.dot(ob_ref[...], wb_ref[1], preferred_element_type=F32)
         + _sigmoid(g2_ref[...].astype(F32)) * jnp.dot(oc, wb_ref[2], preferred_element_type=F32))
    y_ref[...] = y.astype(y_ref.dtype)


def _merge(oa, ob, of, obk, p2, gn, wb):
    T = oa.shape[0]
    D = D_MODEL
    W = 1024
    tm = ROW_BLOCK
    row = lambda n: pl.BlockSpec((tm, n), lambda i: (i, 0))
    return pl.pallas_call(
        _merge_kernel, grid=(T // tm,),
        in_specs=[row(W), row(W), row(W), row(W),
                  pl.BlockSpec((tm, W), lambda i: (i, COL_R // W)),
                  pl.BlockSpec((tm, D), lambda i: (i, 0)),
                  pl.BlockSpec((tm, D), lambda i: (i, 1)),
                  pl.BlockSpec((tm, D), lambda i: (i, 2)),
                  pl.BlockSpec((1, GLA_DV), lambda i: (0, 0)),
                  pl.BlockSpec((3, W, D), lambda i: (0, 0, 0))],
        out_specs=row(D),
        out_shape=jax.ShapeDtypeStruct((T, D), BF16),
        compiler_params=_cparams(("parallel",)), name="merge",
    )(oa, ob, of, obk, p2, p2, p2, p2, gn, wb)


def _layer_norm(z, g, b):
    mu = jnp.mean(z, axis=1, keepdims=True)
    zc = z - mu
    var = jnp.mean(zc * zc, axis=1, keepdims=True)
    return zc * lax.rsqrt(var + LN_EPS) * g + b


def _outln_kernel(y_ref, w_ref, x_ref, g1_ref, sc_ref, sh_ref, lng_ref, lnb_ref, x_out, ht_out, *, alpha):
    mix = jnp.dot(y_ref[...], w_ref[...], preferred_element_type=F32)
    xn = _layer_norm(alpha * x_ref[...] + g1_ref[...] * mix, lng_ref[...], lnb_ref[...])
    x_out[...] = xn
    ht_out[...] = (xn * (1.0 + sc_ref[...]) + sh_ref[...]).T.astype(ht_out.dtype)


def _out_ln(y, w_out, x, mod_l, lng, lnb, nblk, ctx_row, alpha):
    T, D = x.shape
    tm = ROW_BLOCK
    row = pl.BlockSpec((tm, D), lambda i: (i, 0))
    vec = pl.BlockSpec((1, D), lambda i: (0, 0))
    return pl.pallas_call(
        functools.partial(_outln_kernel, alpha=alpha), grid=(T // tm,),
        in_specs=[row, pl.BlockSpec((D, D), lambda i: (0, 0)), row,
                  _mod_spec(2, nblk, ctx_row), _mod_spec(4, nblk, ctx_row), _mod_spec(3, nblk, ctx_row), vec, vec],
        out_specs=[row, pl.BlockSpec((D, tm), lambda i: (0, i))],
        out_shape=[jax.ShapeDtypeStruct((T, D), F32), jax.ShapeDtypeStruct((D, T), BF16)],
        compiler_params=_cparams(("parallel",)), name="out_proj_ln",
    )(y, w_out, x, mod_l, mod_l, mod_l, lng, lnb)


def _ln2_kernel(x_ref, ft_ref, g2_ref, lng_ref, lnb_ref, *rest, alpha, emit_h):
    xn = _layer_norm(alpha * x_ref[...] + g2_ref[...] * ft_ref[...].T, lng_ref[...], lnb_ref[...])
    if emit_h:
        sc_ref, sh_ref, x_out, h_out = rest
        h_out[...] = (xn * (1.0 + sc_ref[...]) + sh_ref[...]).astype(h_out.dtype)
    else:
        (x_out,) = rest
    x_out[...] = xn


def _ln2(x, ffn_t, mod_l, lng, lnb, mod_next, nblk, ctx_row, alpha):
    T, D = x.shape
    tm = ROW_BLOCK
    row = pl.BlockSpec((tm, D), lambda i: (i, 0))
    vec = pl.BlockSpec((1, D), lambda i: (0, 0))
    emit_h = mod_next is not None
    in_specs = [row, pl.BlockSpec((D, tm), lambda i: (0, i)), _mod_spec(5, nblk, ctx_row), vec, vec]
    args = [x, ffn_t, mod_l, lng, lnb]
    out_specs, out_shape = [row], [jax.ShapeDtypeStruct((T, D), F32)]
    if emit_h:
        in_specs += [_mod_spec(1, nblk, ctx_row), _mod_spec(0, nblk, ctx_row)]
        args += [mod_next, mod_next]
        out_specs.append(row)
        out_shape.append(jax.ShapeDtypeStruct((T, D), BF16))
    return pl.pallas_call(
        functools.partial(_ln2_kernel, alpha=alpha, emit_h=emit_h), grid=(T // tm,),
        in_specs=in_specs, out_specs=out_specs, out_shape=out_shape,
        compiler_params=_cparams(("parallel",)), name="ffn_ln",
    )(*args)


def _peer_scores_kernel(h_ref, wq_ref, k1_ref, k2_ref, s_ref):
    qt = jnp.dot(wq_ref[...], h_ref[...], preferred_element_type=F32).astype(BF16)
    n = PEER_NKEYS
    for h in range(PEER_HEADS):
        r = 2 * n * h
        s_ref[r:r + n, :] = jnp.dot(k1_ref[h], qt[r:r + n, :], preferred_element_type=F32)
        s_ref[r + n:r + 2 * n, :] = jnp.dot(k2_ref[h], qt[r + n:r + 2 * n, :], preferred_element_type=F32)


def _peer_scores(ht, wqt, k1, k2):
    D, T = ht.shape
    tt = _pick(T, (512, 256))
    Q = wqt.shape[0]
    n = PEER_NKEYS
    return pl.pallas_call(
        _peer_scores_kernel, grid=(T // tt,),
        in_specs=[pl.BlockSpec((D, tt), lambda i: (0, i)), pl.BlockSpec((Q, D), lambda i: (0, 0)),
                  pl.BlockSpec((PEER_HEADS, n, n), lambda i: (0, 0, 0)),
                  pl.BlockSpec((PEER_HEADS, n, n), lambda i: (0, 0, 0))],
        out_specs=pl.BlockSpec((Q, tt), lambda i: (0, i)),
        out_shape=jax.ShapeDtypeStruct((Q, T), F32),
        compiler_params=_cparams(("parallel",)), name="peer_scores",
    )(ht, wqt, k1, k2)


def _peer_select_kernel(s_ref, nt_ref, e1_ref, s2_ref, e2_ref):
    n, K = PEER_NKEYS, PEER_TOPK
    neg = -jnp.inf

    def top_values(s, count):
        vals = []
        for _ in range(count):
            m = jnp.max(s, axis=0, keepdims=True)
            vals.append(m)
            s = jnp.where(s == m, neg, s)
        return vals

    def body(h, carry):
        r1 = pl.multiple_of(h * 2 * n, 2 * n)
        r2 = pl.multiple_of(h * 2 * n + n, n)
        ro = pl.multiple_of(h * n, n)
        s1 = s_ref[pl.ds(r1, n), :]
        s2 = s_ref[pl.ds(r2, n), :]
        v1 = top_values(s1, K + 1)
        v2 = top_values(s2, K + 1)
        half = K // 2
        v1m = jnp.concatenate(v1[half:K], axis=0)
        v2m = jnp.concatenate(v2[:K], axis=0)
        cand = jnp.concatenate([v1[0] + v2m] + [v1[a] + v2m[:half] for a in range(1, half)]
                               + [v1m + v2[0]], axis=0)
        tops = top_values(cand, K + 1)
        nxt = jnp.maximum(tops[K], jnp.maximum(v1[K] + v2[0], v1[0] + v2[K]))
        tau = 0.5 * (tops[K - 1] + nxt)
        mtop = v1[0] + v2[0]
        z = jnp.sum(jnp.where(cand >= tau, jnp.exp(cand - mtop), 0.0), axis=0, keepdims=True)
        nt_ref[pl.ds(ro, n), :] = tau - s1
        e1_ref[pl.ds(ro, n), :] = jnp.exp(s1 - v1[0]) / z
        s2_ref[pl.ds(ro, n), :] = s2
        e2_ref[pl.ds(ro, n), :] = jnp.exp(s2 - v2[0])
        return carry
    lax.fori_loop(0, PEER_HEADS, body, 0)


def _peer_select(st):
    Q, T = st.shape
    tt = LANES
    R = PEER_HEADS * PEER_NKEYS
    out = pl.BlockSpec((R, tt), lambda i: (0, i))
    return pl.pallas_call(
        _peer_select_kernel, grid=(T // tt,),
        in_specs=[pl.BlockSpec((Q, tt), lambda i: (0, i))],
        out_specs=[out] * 4,
        out_shape=[jax.ShapeDtypeStruct((R, T), F32)] * 4,
        compiler_params=_cparams(("parallel",)), name="peer_select",
    )(st)


def _peer_main_kernel(h_ref, u_ref, vt_ref, nt_ref, e1_ref, s2_ref, e2_ref, o_ref, act_a, act_b, p_a, p_b,
                      *, n_i1, n_e):
    e = pl.program_id(1)
    n = PEER_NKEYS

    @pl.when(e == 0)
    def _():
        o_ref[...] = jnp.zeros_like(o_ref)
        act_b[...] = jnp.zeros_like(act_b)
        p_a[...] = jnp.zeros_like(p_a)

    def step(act_w, act_r, p_w, p_r):
        o_ref[...] += jnp.dot(vt_ref[...], p_r[...], preferred_element_type=F32)

        tile = jnp.clip(e - 1, 0, n_e - 1)
        nr = 4
        rh = n // nr
        for c in range(h_ref.shape[1] // LANES):
            cs = slice(c * LANES, (c + 1) * LANES)
            for r in range(nr):
                acc = [None] * n_i1
                for h in range(PEER_HEADS):
                    rs = slice(h * n + r * rh, h * n + (r + 1) * rh)
                    s2 = s2_ref[rs, cs]
                    e2 = e2_ref[rs, cs]
                    for a in range(n_i1):
                        row = h * n + tile * n_i1 + a
                        ntr = nt_ref[pl.ds(row, 1), :][:, cs]
                        e1r = e1_ref[pl.ds(row, 1), :][:, cs]
                        g = jnp.where(s2 >= ntr, e1r * e2, 0.0)
                        acc[a] = g if acc[a] is None else acc[a] + g
                for a in range(n_i1):
                    ps = slice(a * n + r * rh, a * n + (r + 1) * rh)
                    p_w[ps, cs] = (_gelu(act_r[ps, cs]) * acc[a]).astype(p_w.dtype)

        act_w[...] = jnp.dot(u_ref[...], h_ref[...], preferred_element_type=F32)

    @pl.when(e % 2 == 0)
    def _():
        step(act_a, act_b, p_b, p_a)

    @pl.when(e % 2 == 1)
    def _():
        step(act_b, act_a, p_a, p_b)


def _peer_main(ht, ub, vtb, nt, e1, s2, e2):
    D, T = ht.shape
    E = ub.shape[0]
    tb = _pick(T, (512, 256))
    te = 512
    n_e = E // te
    R = PEER_HEADS * PEER_NKEYS
    sel = pl.BlockSpec((R, tb), lambda i, e: (0, i))
    return pl.pallas_call(
        functools.partial(_peer_main_kernel, n_i1=te // PEER_NKEYS, n_e=n_e), grid=(T // tb, n_e + 2),
        in_specs=[pl.BlockSpec((D, tb), lambda i, e: (0, i)),
                  pl.BlockSpec((te, D), lambda i, e: (jnp.minimum(e, n_e - 1), 0)),
                  pl.BlockSpec((D, te), lambda i, e: (0, jnp.clip(e - 2, 0, n_e - 1))),
                  sel, sel, sel, sel],
        out_specs=pl.BlockSpec((D, tb), lambda i, e: (0, i)),
        out_shape=jax.ShapeDtypeStruct((D, T), F32),
        scratch_shapes=[pltpu.VMEM((te, tb), F32), pltpu.VMEM((te, tb), F32),
                        pltpu.VMEM((te, tb), BF16), pltpu.VMEM((te, tb), BF16)],
        compiler_params=_cparams(("parallel", "arbitrary")), name="peer_main",
    )(ht, ub, vtb, nt, e1, s2, e2)


def _peer(ht, wq, k1, k2, u_tab, v_tab):
    st = _peer_scores(ht, wq.T.astype(BF16), k1.astype(BF16), k2.astype(BF16))
    nt, e1, s2, e2 = _peer_select(st)
    return _peer_main(ht, u_tab.astype(BF16), v_tab.T.astype(BF16), nt, e1, s2, e2)


def kernel(x, c, ctx, c_ctx, w_ada, b_ada, w_in, lam_q1, lam_k1, lam_q2, lam_k2, da_norm_g, s5_a_re, s5_a_im, s5_log_step, s5_b_re, s5_b_im, s5_c_re, s5_c_im, s5_d, s5_glu_w, s5_glu_b, gla_gate_up, gla_gate_b, gla_norm_g, w_branch, w_out, ln1_g, ln1_b, ln2_g, ln2_b, peer_wq, peer_k1, peer_k2, peer_u, peer_v):
    B, SEQ, D = x.shape
    CTX = ctx.shape[1]
    L = w_ada.shape[0]
    assert CTX == ROW_BLOCK and SEQ % ROW_BLOCK == 0 and D == D_MODEL and B < 8
    S = CTX + SEQ
    T = B * S
    nblk = S // ROW_BLOCK
    ctx_row = B
    alpha = (2 * L) ** 0.25

    c8 = jnp.zeros((8, D), F32).at[:B].set(c.astype(F32)).at[B].set(c_ctx.astype(F32))
    mod = _ada_mod(c8, w_ada, b_ada).reshape(L, 8, 6, 1, D)
    tables = _rope_tables(CTX, SEQ)
    xs = jnp.concatenate([ctx, x], axis=1).reshape(T, D).astype(F32)

    sizes = np.cumsum([0, 1024, 1024, 1024, 1024, 512, 512, 1024, 16, 16, 1024, 6144])
    seg = lambda w, k: w[:, sizes[k]:sizes[k + 1]]

    for l in range(L):
        lam_init = 0.8 - 0.6 * math.exp(-0.3 * l)
        mod_l = mod[l]
        wl = w_in[l]
        w_main = jnp.concatenate([seg(wl, 10), seg(wl, 0), seg(wl, 1), seg(wl, 2), seg(wl, 3), seg(wl, 4),
                                  seg(wl, 5), seg(wl, 6), seg(wl, 9)], axis=1).astype(BF16)
        w_gate = jnp.concatenate([seg(wl, 7), seg(wl, 8), jnp.zeros((D, LANES - 2 * GLA_RANK), wl.dtype)],
                                 axis=1).astype(BF16)
        if l == 0:
            h = _modulate(xs, mod_l, nblk, ctx_row)
        p2 = _matmul(h, w_main, BF16, "in_proj")
        pg = _matmul(h, w_gate, BF16, "in_proj_gates")
        p3 = p2.reshape(B, S, N_MAIN)

        lamv = jnp.zeros((8, LANES), F32)
        for r, t in enumerate((lam_q1[l], lam_k1[l], lam_q2[l], lam_k2[l])):
            lamv = lamv.at[r, :DA_QK_DIM].set(t.astype(F32))
        oa = _diff_attention(p3, lamv, da_norm_g[l].astype(F32).reshape(1, DA_V_DIM), tables, lam_init)

        prm = _s5_params(s5_a_re[l], s5_a_im[l], s5_log_step[l], s5_b_re[l], s5_b_im[l],
                         s5_c_re[l], s5_c_im[l], s5_d[l], B)
        ob = _s5_mixer(p3, prm, s5_glu_w[l], s5_glu_b[l])

        gup = jnp.zeros((2, LANES, GLA_HEADS * GLA_DK), F32)
        gup = gup.at[0, :GLA_RANK].set(gla_gate_up[l, 0]).at[1, GLA_RANK:2 * GLA_RANK].set(gla_gate_up[l, 1])
        gbias = gla_gate_b[l].astype(F32).reshape(2, 1, GLA_HEADS * GLA_DK)
        of, obk = _gla(p3, pg.reshape(B, S, LANES), gup.astype(BF16), gbias)

        y = _merge(oa.reshape(T, -1), ob, of.reshape(T, -1), obk.reshape(T, -1), p2,
                   gla_norm_g[l].astype(F32).reshape(1, GLA_DV), w_branch[l].astype(BF16))
        x1, h2t = _out_ln(y, w_out[l].astype(BF16), xs, mod_l, ln1_g[l].reshape(1, D), ln1_b[l].reshape(1, D),
                          nblk, ctx_row, alpha)
        ffn_t = _peer(h2t, peer_wq[l], peer_k1[l], peer_k2[l], peer_u[l], peer_v[l])
        res = _ln2(x1, ffn_t, mod_l, ln2_g[l].reshape(1, D), ln2_b[l].reshape(1, D),
                   mod[l + 1] if l + 1 < L else None, nblk, ctx_row, alpha)
        xs = res[0]
        if l + 1 < L:
            h = res[1]

    return xs.reshape(B, S, D)[:, CTX:, :]
```

```python
import functools
import math

import jax
import jax.numpy as jnp
import numpy as np
from jax import lax
from jax.experimental import pallas as pl
from jax.experimental.pallas import tpu as pltpu

F32 = jnp.float32
BF16 = jnp.bfloat16

D_MODEL = 2048
GRID_W = 64
DA_HEADS = 8
DA_QK_DIM = 64
DA_V_DIM = 128
AX_DIM = DA_QK_DIM // 2
ROPE_BASE = 10000.0
S5_WIDTH = 1024
S5_GROUP = 16
S5_GROUPS = 64
S5_STATE = 64
GLA_HEADS = 4
GLA_DK = 128
GLA_DV = 256
GLA_RANK = 16
GLA_CHUNK = 64
GLA_GATE_NORM = 16.0
PEER_HEADS = 8
PEER_NKEYS = 128
PEER_TOPK = 16
LN_EPS = 1e-5

LANES = 128
ROW_BLOCK = 256
S5_CHUNK = 16
VMEM_LIMIT = 56 * 1024 * 1024

COL_MG = 0
COL_Q = 6144
COL_K = 7168
COL_V = 8192
COL_U = 9216
COL_GQ = 10240
COL_GK = 10752
COL_GV = 11264
COL_R = 12288
N_MAIN = 13312


def _cparams(sem):
    return pltpu.CompilerParams(dimension_semantics=sem, vmem_limit_bytes=VMEM_LIMIT)


def _pick(n, cands):
    for c in cands:
        if n % c == 0:
            return c
    raise ValueError(f"no tile for {n}")


def _gelu(x):
    return 0.5 * x * (1.0 + jnp.tanh(math.sqrt(2.0 / math.pi) * (x + 0.044715 * (x * x * x))))


def _sigmoid(x):
    return 1.0 / (1.0 + jnp.exp(-x))


def _ada_kernel(c_ref, w_ref, b_ref, o_ref):
    c = c_ref[...]
    s = (c * _sigmoid(c)).astype(BF16)
    o_ref[...] = jnp.dot(s, w_ref[...].astype(BF16), preferred_element_type=F32) + b_ref[...]


def _ada_mod(c8, w_ada, b_ada):
    L, D, N = w_ada.shape
    tn = 1536
    return pl.pallas_call(
        _ada_kernel,
        grid=(L, N // tn),
        in_specs=[pl.BlockSpec((8, D), lambda l, j: (0, 0)),
                  pl.BlockSpec((None, D, tn), lambda l, j: (l, 0, j)),
                  pl.BlockSpec((None, 1, tn), lambda l, j: (l, 0, j))],
        out_specs=pl.BlockSpec((None, 8, tn), lambda l, j: (l, 0, j)),
        out_shape=jax.ShapeDtypeStruct((L, 8, N), F32),
        compiler_params=_cparams(("parallel", "parallel")),
        name="ada_mod",
    )(c8, w_ada, b_ada.reshape(L, 1, N))


def _mod_spec(k, nblk, ctx_row):
    return pl.BlockSpec((None, None, 1, D_MODEL),
                        lambda i: (jnp.where(i % nblk == 0, ctx_row, i // nblk), k, 0, 0))


def _modulate_kernel(x_ref, sc_ref, sh_ref, o_ref):
    o_ref[...] = (x_ref[...] * (1.0 + sc_ref[...]) + sh_ref[...]).astype(o_ref.dtype)


def _modulate(x, mod_l, nblk, ctx_row):
    T, D = x.shape
    return pl.pallas_call(
        _modulate_kernel,
        grid=(T // ROW_BLOCK,),
        in_specs=[pl.BlockSpec((ROW_BLOCK, D), lambda i: (i, 0)),
                  _mod_spec(1, nblk, ctx_row), _mod_spec(0, nblk, ctx_row)],
        out_specs=pl.BlockSpec((ROW_BLOCK, D), lambda i: (i, 0)),
        out_shape=jax.ShapeDtypeStruct((T, D), BF16),
        compiler_params=_cparams(("parallel",)),
        name="modulate",
    )(x, mod_l, mod_l)


def _mm_kernel(a_ref, b_ref, o_ref):
    o_ref[...] = jnp.dot(a_ref[...], b_ref[...], preferred_element_type=F32).astype(o_ref.dtype)


def _matmul(a, b, out_dtype, name):
    M, K = a.shape
    N = b.shape[1]
    tm = _pick(M, (1024, 768, 512, 256))
    tn = _pick(N, (1024, 512, 256, 128))
    return pl.pallas_call(
        _mm_kernel,
        grid=(M // tm, N // tn),
        in_specs=[pl.BlockSpec((tm, K), lambda i, j: (i, 0)),
                  pl.BlockSpec((K, tn), lambda i, j: (0, j))],
        out_specs=pl.BlockSpec((tm, tn), lambda i, j: (i, j)),
        out_shape=jax.ShapeDtypeStruct((M, N), out_dtype),
        compiler_params=_cparams(("parallel", "parallel")),
        name=name,
    )(a, b)


def _attn_kernel(lam_ref, g_ref, q_ref, k_ref, v_ref, cos_ref, sa_ref, sb_ref, o_ref, kr_ref, ve_ref,
                 *, lam_init, n_ctx, tk, n_lat, tq):
    qi = pl.program_id(2)
    n_rows = n_ctx + n_lat * tk

    def rope(x, r0, n):
        c = cos_ref[pl.ds(r0, n), :]
        sa = sa_ref[pl.ds(r0, n), :]
        sb = sb_ref[pl.ds(r0, n), :]
        return x * c + pltpu.roll(x, 16, 1) * sa + pltpu.roll(x, LANES - 16, 1) * sb

    @pl.when(qi == 0)
    def _():
        def body(j, carry):
            r0 = pl.multiple_of(j * tq, tq)
            kr_ref[pl.ds(r0, tq), :] = rope(k_ref[pl.ds(r0, tq), :].astype(F32), r0, tq).astype(BF16)
            ve_ref[pl.ds(r0, tq), 0:DA_V_DIM] = v_ref[pl.ds(r0, tq), :]
            ve_ref[pl.ds(r0, tq), DA_V_DIM:2 * DA_V_DIM] = jnp.ones((tq, DA_V_DIM), BF16)
            return carry
        lax.fori_loop(0, n_rows // tq, body, 0)

    lv = lam_ref[...]
    lam = (jnp.exp(jnp.sum(lv[0:1] * lv[1:2], axis=1, keepdims=True))
           - jnp.exp(jnp.sum(lv[2:3] * lv[3:4], axis=1, keepdims=True)) + lam_init)

    q = rope(q_ref[...].astype(F32), pl.multiple_of(qi * tq, tq), tq) * (DA_QK_DIM ** -0.5 * math.log2(math.e))
    lane = lax.broadcasted_iota(jnp.int32, q.shape, 1)
    qs = (jnp.where(lane < DA_QK_DIM, q, 0.0).astype(BF16), jnp.where(lane >= DA_QK_DIM, q, 0.0).astype(BF16))

    def chunk(carry, r0, n):
        kj = kr_ref[r0:r0 + n, :]
        vj = ve_ref[r0:r0 + n, :]
        out = []
        for c in range(2):
            s = lax.dot_general(qs[c], kj, (((1,), (1,)), ((), ())), preferred_element_type=F32)
            mx = jnp.max(s, axis=1, keepdims=True)
            if carry is None:
                mn = mx
                acc = jnp.dot(jnp.exp2(s - mn).astype(BF16), vj, preferred_element_type=F32)
            else:
                m, acc = carry[c]
                mn = jnp.maximum(m, mx)
                acc = jnp.exp2(m - mn) * acc + jnp.dot(jnp.exp2(s - mn).astype(BF16), vj,
                                                      preferred_element_type=F32)
            out.append((mn, acc))
        return out

    def finish(carry):
        (_, acc_a), (_, acc_b) = carry
        o = (acc_a[:, :DA_V_DIM] / acc_a[:, DA_V_DIM:] - lam * (acc_b[:, :DA_V_DIM] / acc_b[:, DA_V_DIM:]))
        ms = jnp.mean(o * o, axis=1, keepdims=True)
        o = o * lax.rsqrt(ms + LN_EPS) * g_ref[...] * (1.0 - lam_init)
        o_ref[...] = o.astype(o_ref.dtype)

    @pl.when(qi == 0)
    def _():
        finish(chunk(None, 0, n_ctx))

    @pl.when(qi > 0)
    def _():
        carry = chunk(None, 0, n_ctx)
        for j in range(n_lat):
            carry = chunk(carry, n_ctx + j * tk, tk)
        finish(carry)


def _rope_tables(ctx_len, seq):
    n = np.arange(seq)
    pos = np.stack([n // GRID_W, n % GRID_W], axis=0).astype(np.float32)
    inv = jnp.asarray(ROPE_BASE, F32) ** (-jnp.arange(AX_DIM // 2, dtype=F32) * 2.0 / AX_DIM)
    lane = np.arange(LANES)
    ax = (lane % DA_QK_DIM) // AX_DIM
    half = (lane % AX_DIM) // (AX_DIM // 2)
    j = lane % (AX_DIM // 2)
    ang = jnp.asarray(pos)[ax, :].T * inv[j][None, :]
    cos, sin = jnp.cos(ang), jnp.sin(ang)
    sa = jnp.where(half[None, :] == 1, sin, 0.0)
    sb = jnp.where(half[None, :] == 0, -sin, 0.0)
    pad = lambda t, v: jnp.concatenate([jnp.full((ctx_len, LANES), v, F32), t.astype(F32)], axis=0)
    return pad(cos, 1.0), pad(sa, 0.0), pad(sb, 0.0)


def _diff_attention(p3, lamv, norm_g, tables, lam_init):
    B, S, _ = p3.shape
    tq = ROW_BLOCK
    n_ctx = ROW_BLOCK
    tk = _pick(S - n_ctx, (1024, 512, 256))
    cq, ck, cv = COL_Q // LANES, COL_K // LANES, COL_V // LANES
    tab_spec = pl.BlockSpec((S, LANES), lambda b, h, i: (0, 0))
    return pl.pallas_call(
        functools.partial(_attn_kernel, lam_init=lam_init, n_ctx=n_ctx, tk=tk, n_lat=(S - n_ctx) // tk, tq=tq),
        grid=(B, DA_HEADS, S // tq),
        in_specs=[pl.BlockSpec((8, LANES), lambda b, h, i: (0, 0)),
                  pl.BlockSpec((1, LANES), lambda b, h, i: (0, 0)),
                  pl.BlockSpec((None, tq, LANES), lambda b, h, i: (b, i, cq + h)),
                  pl.BlockSpec((None, S, LANES), lambda b, h, i: (b, 0, ck + h)),
                  pl.BlockSpec((None, S, LANES), lambda b, h, i: (b, 0, cv + h)),
                  tab_spec, tab_spec, tab_spec],
        out_specs=pl.BlockSpec((None, tq, LANES), lambda b, h, i: (b, i, h)),
        out_shape=jax.ShapeDtypeStruct((B, S, DA_HEADS * DA_V_DIM), BF16),
        scratch_shapes=[pltpu.VMEM((S, LANES), BF16), pltpu.VMEM((S, 2 * DA_V_DIM), BF16)],
        compiler_params=_cparams(("parallel", "parallel", "arbitrary")),
        name="diff_attention",
    )(lamv, norm_g, p3, p3, p3, *tables)


def _s5_params(a_re, a_im, log_step, b_re, b_im, c_re, c_im, d, batch):
    C = S5_CHUNK
    G, P, I = S5_GROUPS, S5_STATE, S5_GROUP
    hp = lax.Precision.HIGHEST
    dt = jnp.exp(log_step.astype(F32))[:, :, None]
    ar, ai = a_re.astype(F32), a_im.astype(F32)
    mag = jnp.exp(ar * dt)
    l_re, l_im = mag * jnp.cos(ai * dt), mag * jnp.sin(ai * dt)
    nr, ni = l_re - 1.0, l_im
    den = ar * ar + ai * ai
    f_re = (nr * ar + ni * ai) / den
    f_im = (ni * ar - nr * ai) / den
    br, bi = b_re.astype(F32), b_im.astype(F32)
    bb_re = f_re[..., None] * br - f_im[..., None] * bi
    bb_im = f_re[..., None] * bi + f_im[..., None] * br
    cr, ci = c_re.astype(F32), c_im.astype(F32)
    kpow = jnp.arange(C + 1, dtype=F32)[None, :, None, None]
    pmag = jnp.exp(kpow * (ar * dt)[:, None])
    pw_re = pmag * jnp.cos(kpow * (ai * dt)[:, None])
    pw_im = pmag * jnp.sin(kpow * (ai * dt)[:, None])
    cp_re = cr[:, None] * pw_re[:, :C, :, None, :] - ci[:, None] * pw_im[:, :C, :, None, :]
    cp_im = cr[:, None] * pw_im[:, :C, :, None, :] + ci[:, None] * pw_re[:, :C, :, None, :]
    kk = (jnp.einsum('dlgop,dgpi->dlgoi', cp_re, bb_re, precision=hp)
          - jnp.einsum('dlgop,dgpi->dlgoi', cp_im, bb_im, precision=hp))
    lag = np.arange(C)[:, None, None]
    s_idx = np.arange(C)[None, :, None]
    j_idx = np.arange(C)[None, None, :]
    sel_f = jnp.asarray((j_idx - s_idx == lag).astype(np.float32))
    sel_b = jnp.asarray((s_idx - j_idx == lag).astype(np.float32))
    mf = jnp.einsum('lsj,lgoi->sjgoi', sel_f, kk[0], precision=hp)
    mb = jnp.einsum('lsj,lgoi->sjgoi', sel_b, kk[1], precision=hp)
    m = (mf + mb).transpose(2, 0, 4, 1, 3).reshape(G, C * I, C * I)
    def wmat(d, pick):
        pr, pi = pick(pw_re[d])[..., None], pick(pw_im[d])[..., None]
        w_re = pr * bb_re[d][None] - pi * bb_im[d][None]
        w_im = pr * bb_im[d][None] + pi * bb_re[d][None]
        t = lambda w: w.transpose(1, 0, 3, 2).reshape(G, C * I, P)
        return jnp.concatenate([t(w_re), t(w_im)], axis=-1)
    w = jnp.concatenate([wmat(0, lambda t: t[:C][::-1]), wmat(1, lambda t: t[:C])], axis=-1)
    def vmat(d, pick):
        pr, pi = pick(pw_re[d])[:, :, None, :], pick(pw_im[d])[:, :, None, :]
        v_re = cr[d][None] * pr - ci[d][None] * pi
        v_im = cr[d][None] * pi + ci[d][None] * pr
        t = lambda v: v.transpose(1, 3, 0, 2).reshape(G, P, C * I)
        return jnp.concatenate([t(v_re), -t(v_im)], axis=1)
    v = jnp.concatenate([vmat(0, lambda t: t[1:]), vmat(1, lambda t: t[1:][::-1])], axis=1)
    lc_re, lc_im = pw_re[:, C], pw_im[:, C]
    def rows(t):
        return jnp.repeat(t, batch, axis=0)
    ca = jnp.stack([rows(jnp.concatenate([lc_re[k], lc_re[k]], -1)) for k in range(2)])
    cb = jnp.stack([rows(jnp.concatenate([-lc_im[k], lc_im[k]], -1)) for k in range(2)])
    dflat = jnp.tile(d.astype(F32).reshape(G, 1, I), (1, C, 1)).reshape(G, 1, C * I)
    return m.astype(BF16), w.astype(BF16), v.astype(BF16), ca, cb, dflat


S5_GBLK = LANES // S5_GROUP


def _s5_lane_perm():
    C, I = S5_CHUNK, S5_GROUP
    r = np.arange(C * LANES)
    s_, g_, i_ = r // LANES, (r % LANES) // I, r % I
    dst = g_ * (C * I) + s_ * I + i_
    return (jnp.asarray(dst)[:, None] == jnp.arange(C * LANES)[None, :]).astype(BF16)


def _s5_sum_kernel(x_ref, perm_ref, w_ref, s_ref, ug_ref):
    K = S5_CHUNK * S5_GROUP
    up = jnp.dot(x_ref[...], perm_ref[...], preferred_element_type=F32).astype(ug_ref.dtype)
    for g in range(S5_GBLK):
        ug = up[:, g * K:(g + 1) * K]
        ug_ref[g] = ug
        s_ref[g] = jnp.dot(ug, w_ref[g], preferred_element_type=F32)


def _s5_summaries(x, perm, w):
    NB, R, KX = x.shape
    K = w.shape[1]
    tr = R // 2
    grp = lambda n: pl.BlockSpec((S5_GBLK, n, K), lambda i, j: (i, 0, 0))
    out = pl.BlockSpec((S5_GBLK, tr, K), lambda i, j: (i, j, 0))
    return pl.pallas_call(
        _s5_sum_kernel, grid=(NB, 2),
        in_specs=[pl.BlockSpec((None, tr, KX), lambda i, j: (i, j, 0)),
                  pl.BlockSpec((KX, KX), lambda i, j: (0, 0)), grp(K)],
        out_specs=[out, out],
        out_shape=[jax.ShapeDtypeStruct((NB * S5_GBLK, R, K), F32), jax.ShapeDtypeStruct((NB * S5_GBLK, R, K), BF16)],
        compiler_params=_cparams(("parallel", "parallel")), name="s5_summaries",
    )(x, perm, w)


def _s5_scan_kernel(sf_ref, sb_ref, ca_ref, cb_ref, xf_out, xb_out, xf_ref, xb_ref, *, nch):
    i = pl.program_id(0)

    @pl.when(i == 0)
    def _():
        xf_ref[...] = jnp.zeros_like(xf_ref)
        xb_ref[...] = jnp.zeros_like(xb_ref)

    caf, cbf, cab, cbb = ca_ref[0], cb_ref[0], ca_ref[1], cb_ref[1]
    xf = xf_ref[...]
    xb = xb_ref[...]
    for j in range(nch):
        xf_out[j] = xf.astype(BF16)
        xf = xf * caf + pltpu.roll(xf, S5_STATE, 1) * cbf + sf_ref[j]
        jb = nch - 1 - j
        xb_out[jb] = xb.astype(BF16)
        xb = xb * cab + pltpu.roll(xb, S5_STATE, 1) * cbb + sb_ref[jb]
    xf_ref[...] = xf
    xb_ref[...] = xb


def _s5_scan(st, ca, cb):
    NC, R, _ = st.shape
    nch = ROW_BLOCK // S5_CHUNK
    nblk = NC // nch
    bwd = lambda i: jnp.where(i == 0, 0, nblk - i)
    P2 = 2 * S5_STATE
    return pl.pallas_call(
        functools.partial(_s5_scan_kernel, nch=nch), grid=(nblk,),
        in_specs=[pl.BlockSpec((nch, R, P2), lambda i: (i, 0, 0)),
                  pl.BlockSpec((nch, R, P2), lambda i: (bwd(i), 0, 1)),
                  pl.BlockSpec((2, R, P2), lambda i: (0, 0, 0)),
                  pl.BlockSpec((2, R, P2), lambda i: (0, 0, 0))],
        out_specs=[pl.BlockSpec((nch, R, P2), lambda i: (i, 0, 0)),
                   pl.BlockSpec((nch, R, P2), lambda i: (bwd(i), 0, 0))],
        out_shape=[jax.ShapeDtypeStruct((NC, R, P2), BF16)] * 2,
        scratch_shapes=[pltpu.VMEM((R, P2), F32), pltpu.VMEM((R, P2), F32)],
        compiler_params=_cparams(("arbitrary",)), name="s5_scan",
    )(st, st, ca, cb)


def _s5_out_kernel(u_ref, xp_ref, m_ref, v_ref, d_ref, perm_ref, o_ref):
    zs = []
    for g in range(S5_GBLK):
        u = u_ref[g]
        y = (jnp.dot(u, m_ref[g], preferred_element_type=F32)
             + jnp.dot(xp_ref[g], v_ref[g], preferred_element_type=F32)
             + u.astype(F32) * d_ref[g])
        zs.append(_gelu(y).astype(o_ref.dtype))
    z = jnp.concatenate(zs, axis=1)
    o_ref[...] = jnp.dot(z, perm_ref[...], preferred_element_type=F32).astype(o_ref.dtype)


def _s5_output(ug, xp, m, v, dflat, perm_t):
    G, R, K = ug.shape
    KX = perm_t.shape[0]
    tr = R // 2
    rows = lambda: pl.BlockSpec((S5_GBLK, tr, K), lambda i, j: (i, j, 0))
    grp = lambda n: pl.BlockSpec((S5_GBLK, n, K), lambda i, j: (i, 0, 0))
    return pl.pallas_call(
        _s5_out_kernel, grid=(G // S5_GBLK, 2),
        in_specs=[rows(), rows(), grp(K), grp(K), grp(1), pl.BlockSpec((KX, KX), lambda i, j: (0, 0))],
        out_specs=pl.BlockSpec((None, tr, KX), lambda i, j: (i, j, 0)),
        out_shape=jax.ShapeDtypeStruct((G // S5_GBLK, R, KX), BF16),
        compiler_params=_cparams(("parallel", "parallel")), name="s5_output",
    )(ug, xp, m, v, dflat, perm_t)


def _glu_kernel(z_ref, w_ref, b_ref, o_ref):
    z = z_ref[...]
    t = jnp.dot(z, w_ref[...], preferred_element_type=F32) + b_ref[...]
    o_ref[...] = (z.astype(F32) * _sigmoid(t)).astype(o_ref.dtype)


def _glu(z, w, b):
    T, N = z.shape
    tm = _pick(T, (1024, 768, 512, 256))
    return pl.pallas_call(
        _glu_kernel, grid=(T // tm,),
        in_specs=[pl.BlockSpec((tm, N), lambda i: (i, 0)), pl.BlockSpec((N, N), lambda i: (0, 0)),
                  pl.BlockSpec((1, N), lambda i: (0, 0))],
        out_specs=pl.BlockSpec((tm, N), lambda i: (i, 0)),
        out_shape=jax.ShapeDtypeStruct((T, N), BF16),
        compiler_params=_cparams(("parallel",)), name="s5_glu",
    )(z, w, b)


def _s5_mixer(p3, prm, glu_w, glu_b):
    B, S, _ = p3.shape
    m, w, v, ca, cb, dflat = prm
    C, G, I = S5_CHUNK, S5_GROUPS, S5_GROUP
    nc = S // C
    u = p3[:, :, COL_U:COL_U + S5_WIDTH]
    nb = S5_WIDTH // LANES
    x = u.reshape(B * nc, C, nb, LANES).transpose(2, 0, 1, 3).reshape(nb, B * nc, C * LANES)
    perm = _s5_lane_perm()
    s, ug = _s5_summaries(x, perm, w)
    st = s.reshape(G, B, nc, 4 * S5_STATE).transpose(2, 0, 1, 3).reshape(nc, G * B, 4 * S5_STATE)
    xf, xb = _s5_scan(st, ca, cb)
    xp = jnp.concatenate([xf, xb], axis=-1).reshape(nc, G, B, 4 * S5_STATE)
    xp = xp.transpose(1, 2, 0, 3).reshape(G, B * nc, 4 * S5_STATE)
    z = _s5_output(ug, xp, m, v, dflat, perm.T)
    z = z.reshape(nb, B * nc, C, LANES).transpose(1, 2, 0, 3).reshape(B * S, S5_WIDTH)
    return _glu(z, glu_w.astype(BF16), glu_b.astype(F32).reshape(1, S5_WIDTH))


def _gla_kernel(qf, kf, vf, gf, qb, kb, vb, gb, gup_ref, gbias_ref, of_ref, ob_ref, sf_ref, sb_ref, *, batch):
    i = pl.program_id(0)
    C = GLA_CHUNK

    @pl.when(i == 0)
    def _():
        sf_ref[...] = jnp.zeros_like(sf_ref)
        sb_ref[...] = jnp.zeros_like(sb_ref)

    row = lax.broadcasted_iota(jnp.int32, (C, C), 0)
    col = lax.broadcasted_iota(jnp.int32, (C, C), 1)
    dn_t = (((1,), (1,)), ((), ()))

    def one(b, d, q_ref, k_ref, v_ref, g_ref, o_ref, s_ref):
        tri = (row >= col) if d == 0 else (row <= col)
        trib = jnp.where(tri, 1.0, 0.0).astype(BF16)
        logits = jnp.dot(g_ref[b], gup_ref[d], preferred_element_type=F32) + gbias_ref[d]
        la = (jnp.minimum(logits, 0.0) - jnp.log(1.0 + jnp.exp(-jnp.abs(logits)))) * (1.0 / GLA_GATE_NORM)
        hi = la.astype(BF16)
        r1 = la - hi.astype(F32)
        mid = r1.astype(BF16)
        lo = (r1 - mid.astype(F32)).astype(BF16)
        bcum = (jnp.dot(trib, hi, preferred_element_type=F32) + jnp.dot(trib, mid, preferred_element_type=F32)
                + jnp.dot(trib, lo, preferred_element_type=F32))
        total = bcum[C - 1:C] if d == 0 else bcum[0:1]
        q = q_ref[b].astype(F32) * (GLA_DK ** -0.5)
        k = k_ref[b].astype(F32)
        qt = (q * jnp.exp(bcum)).astype(BF16)
        kt = (k * jnp.exp(-bcum)).astype(BF16)
        ks = (k * jnp.exp(total - bcum)).astype(BF16)
        et = jnp.exp(total)
        v = v_ref[b]
        for h in range(GLA_HEADS):
            sl = slice(h * GLA_DK, (h + 1) * GLA_DK)
            vs = v[:, h * GLA_DV:(h + 1) * GLA_DV]
            attn = lax.dot_general(qt[:, sl], kt[:, sl], dn_t, preferred_element_type=F32)
            attn = jnp.where(tri, attn, 0.0).astype(BF16)
            st = s_ref[b * GLA_HEADS + h]
            o = (jnp.dot(attn, vs, preferred_element_type=F32)
                 + lax.dot_general(qt[:, sl], st.astype(BF16), dn_t, preferred_element_type=F32))
            o_ref[b, :, h * GLA_DV:(h + 1) * GLA_DV] = o
            upd = lax.dot_general(vs, ks[:, sl], (((0,), (0,)), ((), ())), preferred_element_type=F32)
            s_ref[b * GLA_HEADS + h] = st * et[:, sl] + upd

    def body(b, carry):
        one(b, 0, qf, kf, vf, gf, of_ref, sf_ref)
        one(b, 1, qb, kb, vb, gb, ob_ref, sb_ref)
        return carry
    lax.fori_loop(0, batch, body, 0)


def _gla(p3, pg3, gup, gbias):
    B, S, _ = p3.shape
    C = GLA_CHUNK
    nch = S // C
    nctx = ROW_BLOCK // C
    bwd = lambda i: jnp.where(i < nctx, nctx - 1 - i, nch + nctx - 1 - i)
    wq, wv = GLA_HEADS * GLA_DK, GLA_HEADS * GLA_DV
    def specs(f):
        return [pl.BlockSpec((B, C, wq), lambda i: (0, f(i), COL_GQ // wq)),
                pl.BlockSpec((B, C, wq), lambda i: (0, f(i), COL_GK // wq)),
                pl.BlockSpec((B, C, wv), lambda i: (0, f(i), COL_GV // wv)),
                pl.BlockSpec((B, C, LANES), lambda i: (0, f(i), 0))]
    fwd = lambda i: i
    return pl.pallas_call(
        functools.partial(_gla_kernel, batch=B), grid=(nch,),
        in_specs=specs(fwd) + specs(bwd) + [pl.BlockSpec((2, LANES, wq), lambda i: (0, 0, 0)),
                                            pl.BlockSpec((2, 1, wq), lambda i: (0, 0, 0))],
        out_specs=[pl.BlockSpec((B, C, wv), lambda i: (0, i, 0)),
                   pl.BlockSpec((B, C, wv), lambda i: (0, bwd(i), 0))],
        out_shape=[jax.ShapeDtypeStruct((B, S, wv), F32)] * 2,
        scratch_shapes=[pltpu.VMEM((B * GLA_HEADS, GLA_DV, GLA_DK), F32)] * 2,
        compiler_params=_cparams(("arbitrary",)), name="gla",
    )(p3, p3, p3, pg3, p3, p3, p3, pg3, gup, gbias)


def _merge_kernel(oa_ref, ob_ref, of_ref, obk_ref, r_ref, g0_ref, g1_ref, g2_ref, gn_ref, wb_ref, y_ref):
    o = of_ref[...] + obk_ref[...]
    gn = gn_ref[...]
    parts = []
    for h in range(GLA_HEADS):
        oh = o[:, h * GLA_DV:(h + 1) * GLA_DV]
        ms = jnp.mean(oh * oh, axis=1, keepdims=True)
        parts.append(oh * lax.rsqrt(ms + LN_EPS) * gn)
    r = r_ref[...].astype(F32)
    oc = (jnp.concatenate(parts, axis=1) * (r * _sigmoid(r))).astype(BF16)
    y = (_sigmoid(g0_ref[...].astype(F32)) * jnp.dot(oa_ref[...], wb_ref[0], preferred_element_type=F32)
         + _sigmoid(g1_ref[...].astype(F32)) * jnp.dot(ob_ref[...], wb_ref[1], preferred_element_type=F32)
         + _sigmoid(g2_ref[...].astype(F32)) * jnp.dot(oc, wb_ref[2], preferred_element_type=F32))
    y_ref[...] = y.astype(y_ref.dtype)


def _merge(oa, ob, of, obk, p2, gn, wb):
    T = oa.shape[0]
    D = D_MODEL
    W = 1024
    tm = ROW_BLOCK
    row = lambda n: pl.BlockSpec((tm, n), lambda i: (i, 0))
    return pl.pallas_call(
        _merge_kernel, grid=(T // tm,),
        in_specs=[row(W), row(W), row(W), row(W),
                  pl.BlockSpec((tm, W), lambda i: (i, COL_R // W)),
                  pl.BlockSpec((tm, D), lambda i: (i, 0)),
                  pl.BlockSpec((tm, D), lambda i: (i, 1)),
                  pl.BlockSpec((tm, D), lambda i: (i, 2)),
                  pl.BlockSpec((1, GLA_DV), lambda i: (0, 0)),
                  pl.BlockSpec((3, W, D), lambda i: (0, 0, 0))],
        out_specs=row(D),
        out_shape=jax.ShapeDtypeStruct((T, D), BF16),
        compiler_params=_cparams(("parallel",)), name="merge",
    )(oa, ob, of, obk, p2, p2, p2, p2, gn, wb)


def _layer_norm(z, g, b):
    mu = jnp.mean(z, axis=1, keepdims=True)
    zc = z - mu
    var = jnp.mean(zc * zc, axis=1, keepdims=True)
    return zc * lax.rsqrt(var + LN_EPS) * g + b


def _outln_kernel(y_ref, w_ref, x_ref, g1_ref, sc_ref, sh_ref, lng_ref, lnb_ref, x_out, ht_out, *, alpha):
    mix = jnp.dot(y_ref[...], w_ref[...], preferred_element_type=F32)
    xn = _layer_norm(alpha * x_ref[...] + g1_ref[...] * mix, lng_ref[...], lnb_ref[...])
    x_out[...] = xn
    ht_out[...] = (xn * (1.0 + sc_ref[...]) + sh_ref[...]).T.astype(ht_out.dtype)


def _out_ln(y, w_out, x, mod_l, lng, lnb, nblk, ctx_row, alpha):
    T, D = x.shape
    tm = ROW_BLOCK
    row = pl.BlockSpec((tm, D), lambda i: (i, 0))
    vec = pl.BlockSpec((1, D), lambda i: (0, 0))
    return pl.pallas_call(
        functools.partial(_outln_kernel, alpha=alpha), grid=(T // tm,),
        in_specs=[row, pl.BlockSpec((D, D), lambda i: (0, 0)), row,
                  _mod_spec(2, nblk, ctx_row), _mod_spec(4, nblk, ctx_row), _mod_spec(3, nblk, ctx_row), vec, vec],
        out_specs=[row, pl.BlockSpec((D, tm), lambda i: (0, i))],
        out_shape=[jax.ShapeDtypeStruct((T, D), F32), jax.ShapeDtypeStruct((D, T), BF16)],
        compiler_params=_cparams(("parallel",)), name="out_proj_ln",
    )(y, w_out, x, mod_l, mod_l, mod_l, lng, lnb)


def _ln2_kernel(x_ref, ft_ref, g2_ref, lng_ref, lnb_ref, *rest, alpha, emit_h):
    xn = _layer_norm(alpha * x_ref[...] + g2_ref[...] * ft_ref[...].T, lng_ref[...], lnb_ref[...])
    if emit_h:
        sc_ref, sh_ref, x_out, h_out = rest
        h_out[...] = (xn * (1.0 + sc_ref[...]) + sh_ref[...]).astype(h_out.dtype)
    else:
        (x_out,) = rest
    x_out[...] = xn


def _ln2(x, ffn_t, mod_l, lng, lnb, mod_next, nblk, ctx_row, alpha):
    T, D = x.shape
    tm = ROW_BLOCK
    row = pl.BlockSpec((tm, D), lambda i: (i, 0))
    vec = pl.BlockSpec((1, D), lambda i: (0, 0))
    emit_h = mod_next is not None
    in_specs = [row, pl.BlockSpec((D, tm), lambda i: (0, i)), _mod_spec(5, nblk, ctx_row), vec, vec]
    args = [x, ffn_t, mod_l, lng, lnb]
    out_specs, out_shape = [row], [jax.ShapeDtypeStruct((T, D), F32)]
    if emit_h:
        in_specs += [_mod_spec(1, nblk, ctx_row), _mod_spec(0, nblk, ctx_row)]
        args += [mod_next, mod_next]
        out_specs.append(row)
        out_shape.append(jax.ShapeDtypeStruct((T, D), BF16))
    return pl.pallas_call(
        functools.partial(_ln2_kernel, alpha=alpha, emit_h=emit_h), grid=(T // tm,),
        in_specs=in_specs, out_specs=out_specs, out_shape=out_shape,
        compiler_params=_cparams(("parallel",)), name="ffn_ln",
    )(*args)


def _peer_scores_kernel(h_ref, wq_ref, k1_ref, k2_ref, s_ref):
    qt = jnp.dot(wq_ref[...], h_ref[...], preferred_element_type=F32).astype(BF16)
    n = PEER_NKEYS
    for h in range(PEER_HEADS):
        r = 2 * n * h
        s_ref[r:r + n, :] = jnp.dot(k1_ref[h], qt[r:r + n, :], preferred_element_type=F32)
        s_ref[r + n:r + 2 * n, :] = jnp.dot(k2_ref[h], qt[r + n:r + 2 * n, :], preferred_element_type=F32)


def _peer_scores(ht, wqt, k1, k2):
    D, T = ht.shape
    tt = _pick(T, (512, 256))
    Q = wqt.shape[0]
    n = PEER_NKEYS
    return pl.pallas_call(
        _peer_scores_kernel, grid=(T // tt,),
        in_specs=[pl.BlockSpec((D, tt), lambda i: (0, i)), pl.BlockSpec((Q, D), lambda i: (0, 0)),
                  pl.BlockSpec((PEER_HEADS, n, n), lambda i: (0, 0, 0)),
                  pl.BlockSpec((PEER_HEADS, n, n), lambda i: (0, 0, 0))],
        out_specs=pl.BlockSpec((Q, tt), lambda i: (0, i)),
        out_shape=jax.ShapeDtypeStruct((Q, T), F32),
        compiler_params=_cparams(("parallel",)), name="peer_scores",
    )(ht, wqt, k1, k2)


def _peer_select_kernel(s_ref, nt_ref, e1_ref, s2_ref, e2_ref):
    n, K = PEER_NKEYS, PEER_TOPK
    neg = -jnp.inf

    def top_values(s, count):
        vals = []
        for _ in range(count):
            m = jnp.max(s, axis=0, keepdims=True)
            vals.append(m)
            s = jnp.where(s == m, neg, s)
        return vals

    def body(h, carry):
        r1 = pl.multiple_of(h * 2 * n, 2 * n)
        r2 = pl.multiple_of(h * 2 * n + n, n)
        ro = pl.multiple_of(h * n, n)
        s1 = s_ref[pl.ds(r1, n), :]
        s2 = s_ref[pl.ds(r2, n), :]
        v1 = top_values(s1, K + 1)
        v2 = top_values(s2, K + 1)
        half = K // 2
        v1m = jnp.concatenate(v1[half:K], axis=0)
        v2m = jnp.concatenate(v2[:K], axis=0)
        cand = jnp.concatenate([v1[0] + v2m] + [v1[a] + v2m[:half] for a in range(1, half)]
                               + [v1m + v2[0]], axis=0)
        tops = top_values(cand, K + 1)
        nxt = jnp.maximum(tops[K], jnp.maximum(v1[K] + v2[0], v1[0] + v2[K]))
        tau = 0.5 * (tops[K - 1] + nxt)
        mtop = v1[0] + v2[0]
        z = jnp.sum(jnp.where(cand >= tau, jnp.exp(cand - mtop), 0.0), axis=0, keepdims=True)
        nt_ref[pl.ds(ro, n), :] = tau - s1
        e1_ref[pl.ds(ro, n), :] = jnp.exp(s1 - v1[0]) / z
        s2_ref[pl.ds(ro, n), :] = s2
        e2_ref[pl.ds(ro, n), :] = jnp.exp(s2 - v2[0])
        return carry
    lax.fori_loop(0, PEER_HEADS, body, 0)


def _peer_select(st):
    Q, T = st.shape
    tt = _pick(T, (512, 256, 128))
    R = PEER_HEADS * PEER_NKEYS
    out = pl.BlockSpec((R, tt), lambda i: (0, i))
    return pl.pallas_call(
        _peer_select_kernel, grid=(T // tt,),
        in_specs=[pl.BlockSpec((Q, tt), lambda i: (0, i))],
        out_specs=[out] * 4,
        out_shape=[jax.ShapeDtypeStruct((R, T), F32)] * 4,
        compiler_params=_cparams(("parallel",)), name="peer_select",
    )(st)


def _peer_main_kernel(h_ref, u_ref, vt_ref, nt_ref, e1_ref, s2_ref, e2_ref, o_ref, act_a, act_b, p_a, p_b,
                      *, n_i1, n_e):
    e = pl.program_id(1)
    n = PEER_NKEYS

    @pl.when(e == 0)
    def _():
        o_ref[...] = jnp.zeros_like(o_ref)
        act_b[...] = jnp.zeros_like(act_b)
        p_a[...] = jnp.zeros_like(p_a)

    def step(act_w, act_r, p_w, p_r):
        o_ref[...] += jnp.dot(vt_ref[...], p_r[...], preferred_element_type=F32)

        tile = jnp.clip(e - 1, 0, n_e - 1)
        nr = 4
        rh = n // nr
        for c in range(h_ref.shape[1] // LANES):
            cs = slice(c * LANES, (c + 1) * LANES)
            for r in range(nr):
                acc = [None] * n_i1
                for h in range(PEER_HEADS):
                    rs = slice(h * n + r * rh, h * n + (r + 1) * rh)
                    s2 = s2_ref[rs, cs]
                    e2 = e2_ref[rs, cs]
                    for a in range(n_i1):
                        row = h * n + tile * n_i1 + a
                        ntr = nt_ref[pl.ds(row, 1), :][:, cs]
                        e1r = e1_ref[pl.ds(row, 1), :][:, cs]
                        g = jnp.where(s2 >= ntr, e1r * e2, 0.0)
                        acc[a] = g if acc[a] is None else acc[a] + g
                for a in range(n_i1):
                    ps = slice(a * n + r * rh, a * n + (r + 1) * rh)
                    p_w[ps, cs] = (_gelu(act_r[ps, cs]) * acc[a]).astype(p_w.dtype)

        act_w[...] = jnp.dot(u_ref[...], h_ref[...], preferred_element_type=F32)

    @pl.when(e % 2 == 0)
    def _():
        step(act_a, act_b, p_b, p_a)

    @pl.when(e % 2 == 1)
    def _():
        step(act_b, act_a, p_a, p_b)


def _peer_main(ht, ub, vtb, nt, e1, s2, e2):
    D, T = ht.shape
    E = ub.shape[0]
    tb = _pick(T, (512, 256))
    te = 512
    n_e = E // te
    R = PEER_HEADS * PEER_NKEYS
    sel = pl.BlockSpec((R, tb), lambda i, e: (0, i))
    return pl.pallas_call(
        functools.partial(_peer_main_kernel, n_i1=te // PEER_NKEYS, n_e=n_e), grid=(T // tb, n_e + 2),
        in_specs=[pl.BlockSpec((D, tb), lambda i, e: (0, i)),
                  pl.BlockSpec((te, D), lambda i, e: (jnp.minimum(e, n_e - 1), 0)),
                  pl.BlockSpec((D, te), lambda i, e: (0, jnp.clip(e - 2, 0, n_e - 1))),
                  sel, sel, sel, sel],
        out_specs=pl.BlockSpec((D, tb), lambda i, e: (0, i)),
        out_shape=jax.ShapeDtypeStruct((D, T), F32),
        scratch_shapes=[pltpu.VMEM((te, tb), F32), pltpu.VMEM((te, tb), F32),
                        pltpu.VMEM((te, tb), BF16), pltpu.VMEM((te, tb), BF16)],
        compiler_params=_cparams(("parallel", "arbitrary")), name="peer_main",
    )(ht, ub, vtb, nt, e1, s2, e2)


def _peer(ht, wq, k1, k2, u_tab, v_tab):
    st = _peer_scores(ht, wq.T.astype(BF16), k1.astype(BF16), k2.astype(BF16))
    nt, e1, s2, e2 = _peer_select(st)
    return _peer_main(ht, u_tab.astype(BF16), v_tab.T.astype(BF16), nt, e1, s2, e2)


def kernel(x, c, ctx, c_ctx, w_ada, b_ada, w_in, lam_q1, lam_k1, lam_q2, lam_k2, da_norm_g, s5_a_re, s5_a_im, s5_log_step, s5_b_re, s5_b_im, s5_c_re, s5_c_im, s5_d, s5_glu_w, s5_glu_b, gla_gate_up, gla_gate_b, gla_norm_g, w_branch, w_out, ln1_g, ln1_b, ln2_g, ln2_b, peer_wq, peer_k1, peer_k2, peer_u, peer_v):
    B, SEQ, D = x.shape
    CTX = ctx.shape[1]
    L = w_ada.shape[0]
    assert CTX == ROW_BLOCK and SEQ % ROW_BLOCK == 0 and D == D_MODEL and B < 8
    S = CTX + SEQ
    T = B * S
    nblk = S // ROW_BLOCK
    ctx_row = B
    alpha = (2 * L) ** 0.25

    c8 = jnp.zeros((8, D), F32).at[:B].set(c.astype(F32)).at[B].set(c_ctx.astype(F32))
    mod = _ada_mod(c8, w_ada, b_ada).reshape(L, 8, 6, 1, D)
    tables = _rope_tables(CTX, SEQ)
    xs = jnp.concatenate([ctx, x], axis=1).reshape(T, D).astype(F32)

    sizes = np.cumsum([0, 1024, 1024, 1024, 1024, 512, 512, 1024, 16, 16, 1024, 6144])
    seg = lambda w, k: w[:, sizes[k]:sizes[k + 1]]

    for l in range(L):
        lam_init = 0.8 - 0.6 * math.exp(-0.3 * l)
        mod_l = mod[l]
        wl = w_in[l]
        w_main = jnp.concatenate([seg(wl, 10), seg(wl, 0), seg(wl, 1), seg(wl, 2), seg(wl, 3), seg(wl, 4),
                                  seg(wl, 5), seg(wl, 6), seg(wl, 9)], axis=1).astype(BF16)
        w_gate = jnp.concatenate([seg(wl, 7), seg(wl, 8), jnp.zeros((D, LANES - 2 * GLA_RANK), wl.dtype)],
                                 axis=1).astype(BF16)
        if l == 0:
            h = _modulate(xs, mod_l, nblk, ctx_row)
        p2 = _matmul(h, w_main, BF16, "in_proj")
        pg = _matmul(h, w_gate, BF16, "in_proj_gates")
        p3 = p2.reshape(B, S, N_MAIN)

        lamv = jnp.zeros((8, LANES), F32)
        for r, t in enumerate((lam_q1[l], lam_k1[l], lam_q2[l], lam_k2[l])):
            lamv = lamv.at[r, :DA_QK_DIM].set(t.astype(F32))
        oa = _diff_attention(p3, lamv, da_norm_g[l].astype(F32).reshape(1, DA_V_DIM), tables, lam_init)

        prm = _s5_params(s5_a_re[l], s5_a_im[l], s5_log_step[l], s5_b_re[l], s5_b_im[l],
                         s5_c_re[l], s5_c_im[l], s5_d[l], B)
        ob = _s5_mixer(p3, prm, s5_glu_w[l], s5_glu_b[l])

        gup = jnp.zeros((2, LANES, GLA_HEADS * GLA_DK), F32)
        gup = gup.at[0, :GLA_RANK].set(gla_gate_up[l, 0]).at[1, GLA_RANK:2 * GLA_RANK].set(gla_gate_up[l, 1])
        gbias = gla_gate_b[l].astype(F32).reshape(2, 1, GLA_HEADS * GLA_DK)
        of, obk = _gla(p3, pg.reshape(B, S, LANES), gup.astype(BF16), gbias)

        y = _merge(oa.reshape(T, -1), ob, of.reshape(T, -1), obk.reshape(T, -1), p2,
                   gla_norm_g[l].astype(F32).reshape(1, GLA_DV), w_branch[l].astype(BF16))
        x1, h2t = _out_ln(y, w_out[l].astype(BF16), xs, mod_l, ln1_g[l].reshape(1, D), ln1_b[l].reshape(1, D),
                          nblk, ctx_row, alpha)
        ffn_t = _peer(h2t, peer_wq[l], peer_k1[l], peer_k2[l], peer_u[l], peer_v[l])
        res = _ln2(x1, ffn_t, mod_l, ln2_g[l].reshape(1, D), ln2_b[l].reshape(1, D),
                   mod[l + 1] if l + 1 < L else None, nblk, ctx_row, alpha)
        xs = res[0]
        if l + 1 < L:
            h = res[1]

    return xs.reshape(B, S, D)[:, CTX:, :]
```
